```python
import math
import jax, jax.numpy as jnp
from jax import lax
import numpy as np

D_MODEL = 2048
BATCH = 2
SEQ = 16384
DEPTH = 2

GRID_W = 64
CTX_LEN = 256
HEAD_DIM = 128
A_HEADS = 8
B_HEADS = 8
B_KV_HEADS = 2
B_GROUP = B_HEADS // B_KV_HEADS
C_W = D_MODEL // 2
D_HEADS = 4
NA_ROWS = 8
NA_COLS = 16
ROPE_THETA = 10000.0
D_FF = 5632
Q_BLOCK = 128
EPS = 1e-6
N_EVEN = (DEPTH + 1) // 2
N_ODD = DEPTH // 2

A_W = A_HEADS * HEAD_DIM
B_W = B_HEADS * HEAD_DIM
B_KV_W = B_KV_HEADS * HEAD_DIM
AB_KV_START = A_W + B_W
AB_COLS = AB_KV_START + 2 * A_W + 2 * B_KV_W
D_QW = D_HEADS * 2 * HEAD_DIM
D_VW = D_HEADS * 2 * HEAD_DIM
CD_KV_START = 3 * C_W + D_QW
CD_COLS = CD_KV_START + D_QW + D_VW
F32 = jnp.float32

kernel_name = 'hybrid_natten_gqa_shortconv_diffattn_dit'


def rms_norm(x, g):
    xf = x.astype(F32)
    y = xf * lax.rsqrt(jnp.mean(xf * xf, axis=-1, keepdims=True) + EPS)
    return (y * g.astype(F32)).astype(x.dtype)


def modulate(h, shift, scale):
    return h * (1 + scale) + shift


def dwconv3(x, w):
    xp = jnp.pad(x, ((0, 0), (1, 1), (0, 0)))
    return xp[:, :-2] * w[0] + xp[:, 1:-1] * w[1] + xp[:, 2:] * w[2]


def axial_rope_tables(n_tokens):
    t = jnp.arange(n_tokens)
    axis_dim = HEAD_DIM // 2
    inv = 1.0 / (ROPE_THETA ** (jnp.arange(0, axis_dim, 2, dtype=F32) / axis_dim))
    ang_r = (t // GRID_W).astype(F32)[:, None] * inv
    ang_c = (t % GRID_W).astype(F32)[:, None] * inv
    ang = jnp.concatenate([ang_r, ang_r, ang_c, ang_c], axis=-1)
    return jnp.cos(ang), jnp.sin(ang)


def apply_rope(x, cos, sin):
    shape = (1, x.shape[1]) + (1,) * (x.ndim - 3) + (x.shape[-1],)
    xf = x.astype(F32)
    x1, x2, x3, x4 = jnp.split(xf, 4, axis=-1)
    rot = jnp.concatenate([-x2, x1, -x4, x3], axis=-1)
    return (xf * cos.reshape(shape) + rot * sin.reshape(shape)).astype(x.dtype)


def sweep_query_blocks(fn, q):
    b, s = q.shape[:2]
    qb = jnp.moveaxis(q.reshape((b, s // Q_BLOCK, Q_BLOCK) + q.shape[2:]), 1, 0)
    out = jnp.moveaxis(lax.map(fn, qb), 0, 1)
    return out.reshape((b, s) + out.shape[3:])


def gqa_attention(q, k, v):
    scale = q.shape[-1] ** -0.5

    def block(qb):
        s = jnp.einsum('bqgrd,bkgd->bgrqk', qb, k).astype(F32) * scale
        p = jax.nn.softmax(s, axis=-1).astype(v.dtype)
        return jnp.einsum('bgrqk,bkgd->bqgrd', p, v)

    return sweep_query_blocks(block, q)


def diff_attention(q, k, v, lam):
    scale = q.shape[-1] ** -0.5

    def block(qb):
        s = jnp.einsum('bqhcd,bkhcd->bhcqk', qb, k).astype(F32) * scale
        p = jax.nn.softmax(s, axis=-1)
        a = (p[:, :, 0] - lam * p[:, :, 1]).astype(v.dtype)
        return jnp.einsum('bhqk,bkhe->bqhe', a, v)

    return sweep_query_blocks(block, q)


def neighbourhood_attention(q, k, v, kc, vc, rpb):
    b, s, h, d = q.shape
    rows = s // GRID_W
    kh = min(NA_ROWS, rows)
    kw = NA_COLS
    scale = d ** -0.5
    qg = q.reshape(b, rows, GRID_W, h, d)
    kg = k.reshape(b, rows, GRID_W, h, d)
    vg = v.reshape(b, rows, GRID_W, h, d)
    col = jnp.arange(GRID_W)
    col_start = jnp.clip(col - kw // 2, 0, GRID_W - kw)
    col_idx = col_start[:, None] + jnp.arange(kw)[None, :]
    col_bias_idx = col_idx - col[:, None] + NA_COLS - 1

    def row_block(r):
        r0 = jnp.clip(r - kh // 2, 0, rows - kh)
        kn = lax.dynamic_slice_in_dim(kg, r0, kh, axis=1)[:, :, col_idx]
        vn = lax.dynamic_slice_in_dim(vg, r0, kh, axis=1)[:, :, col_idx]
        qr = lax.dynamic_index_in_dim(qg, r, axis=1, keepdims=False)
        row_bias_idx = r0 + jnp.arange(kh) - r + NA_ROWS - 1
        bias = rpb[:, row_bias_idx[:, None, None], col_bias_idx[None, :, :]]
        s_nb = jnp.einsum('bqhd,bjqwhd->bhqjw', qr, kn).astype(F32) * scale
        s_nb = s_nb + jnp.transpose(bias, (0, 2, 1, 3)).astype(F32)[None]
        s_ctx = jnp.einsum('bqhd,blhd->bhql', qr, kc).astype(F32) * scale
        sc = jnp.concatenate([s_nb.reshape(b, h, GRID_W, kh * kw), s_ctx], axis=-1)
        p = jax.nn.softmax(sc, axis=-1).astype(v.dtype)
        p_nb = p[..., :kh * kw].reshape(b, h, GRID_W, kh, kw)
        return (jnp.einsum('bhqjw,bjqwhd->bqhd', p_nb, vn)
                + jnp.einsum('bhql,blhd->bqhd', p[..., kh * kw:], vc))

    out = lax.map(row_block, jnp.arange(rows))
    return jnp.moveaxis(out, 0, 1).reshape(b, s, h, d)


def ab_kv(p_kv):
    a_k, a_v, b_k, b_v = jnp.split(p_kv, [A_W, 2 * A_W, 2 * A_W + B_KV_W], axis=-1)
    lead = p_kv.shape[:2]
    return (a_k.reshape(lead + (A_HEADS, HEAD_DIM)), a_v.reshape(lead + (A_HEADS, HEAD_DIM)),
            b_k.reshape(lead + (B_KV_HEADS, HEAD_DIM)), b_v.reshape(lead + (B_KV_HEADS, HEAD_DIM)))


def mixer_ab(h_lat, h_ctx, w_in, w_out, rpb, q_gain, k_gain, cos, sin, ctx_out):
    b, s, _ = h_lat.shape
    p = h_lat @ w_in
    a_q = p[..., :A_W].reshape(b, s, A_HEADS, HEAD_DIM)
    b_q = p[..., A_W:AB_KV_START].reshape(b, s, B_KV_HEADS, B_GROUP, HEAD_DIM)
    a_k, a_v, b_k, b_v = ab_kv(p[..., AB_KV_START:])
    pc = h_ctx @ (w_in if ctx_out else w_in[:, AB_KV_START:])
    ac_k, ac_v, bc_k, bc_v = ab_kv(pc[..., AB_KV_START:] if ctx_out else pc)
    bc_k = rms_norm(bc_k, k_gain)
    b_q = apply_rope(rms_norm(b_q, q_gain), cos, sin)
    b_k = apply_rope(rms_norm(b_k, k_gain), cos, sin)
    o_a = neighbourhood_attention(a_q, a_k, a_v, ac_k, ac_v, rpb)
    o_b = gqa_attention(b_q, jnp.concatenate([bc_k, b_k], axis=1), jnp.concatenate([bc_v, b_v], axis=1))
    y_lat = jnp.concatenate([o_a.reshape(b, s, A_W), o_b.reshape(b, s, B_W)], axis=-1) @ w_out
    if not ctx_out:
        return y_lat, None
    l = h_ctx.shape[1]
    ac_q = pc[..., :A_W].reshape(b, l, A_HEADS, 1, HEAD_DIM)
    bc_q = rms_norm(pc[..., A_W:AB_KV_START].reshape(b, l, B_KV_HEADS, B_GROUP, HEAD_DIM), q_gain)
    oc_a = gqa_attention(ac_q, ac_k, ac_v)
    oc_b = gqa_attention(bc_q, bc_k, bc_v)
    y_ctx = jnp.concatenate([oc_a.reshape(b, l, A_W), oc_b.reshape(b, l, B_W)], axis=-1) @ w_out
    return y_lat, y_ctx


def short_conv(p_c, conv_w):
    u, g_b, g_c = jnp.split(p_c, 3, axis=-1)
    return g_b * dwconv3(g_c * u, conv_w)


def diff_kv(p_kv):
    lead = p_kv.shape[:2]
    k = p_kv[..., :D_QW].reshape(lead + (D_HEADS, 2, HEAD_DIM))
    v = p_kv[..., D_QW:].reshape(lead + (D_HEADS, 2 * HEAD_DIM))
    return k, v


def mixer_cd(h_lat, h_ctx, w_in, w_out, conv_w, lam, lam_init, subln_g, cos, sin, ctx_out):
    b, s, _ = h_lat.shape
    p = h_lat @ w_in
    y_c = short_conv(p[..., :3 * C_W], conv_w)
    d_q = apply_rope(p[..., 3 * C_W:CD_KV_START].reshape(b, s, D_HEADS, 2, HEAD_DIM), cos, sin)
    d_k, d_v = diff_kv(p[..., CD_KV_START:])
    d_k = apply_rope(d_k, cos, sin)
    pc = h_ctx @ (w_in if ctx_out else w_in[:, CD_KV_START:])
    dc_k, dc_v = diff_kv(pc[..., CD_KV_START:] if ctx_out else pc)
    o_d = diff_attention(d_q, jnp.concatenate([dc_k, d_k], axis=1), jnp.concatenate([dc_v, d_v], axis=1), lam)
    o_d = rms_norm(o_d, subln_g) * (1.0 - lam_init)
    y_lat = jnp.concatenate([y_c, o_d.reshape(b, s, D_VW)], axis=-1) @ w_out
    if not ctx_out:
        return y_lat, None
    l = h_ctx.shape[1]
    yc_c = short_conv(pc[..., :3 * C_W], conv_w)
    dc_q = pc[..., 3 * C_W:CD_KV_START].reshape(b, l, D_HEADS, 2, HEAD_DIM)
    oc_d = rms_norm(diff_attention(dc_q, dc_k, dc_v, lam), subln_g) * (1.0 - lam_init)
    y_ctx = jnp.concatenate([yc_c, oc_d.reshape(b, l, D_VW)], axis=-1) @ w_out
    return y_lat, y_ctx


def conv_glu(h, w_gate, w_up, conv_w, conv_b, w_down):
    g = dwconv3(h @ w_gate, conv_w) + conv_b
    return (jax.nn.silu(g) * (h @ w_up)) @ w_down


def setup_inputs(seed: int = 0) -> dict:
    key = jax.random.key(seed)
    ks = jax.random.split(key, 26)
    D = D_MODEL

    def nrm(k, shape, s):
        return jax.random.normal(k, shape, F32) * s

    return {
        'x': nrm(ks[0], (BATCH, SEQ, D), 1.0),
        'c': nrm(ks[1], (BATCH, D), 1.0),
        'ctx': nrm(ks[2], (BATCH, CTX_LEN, D), 1.0),
        'c_ctx': nrm(ks[3], (D,), 1.0),
        'ada_w': nrm(ks[4], (DEPTH, D, 6 * D), 0.5 * D ** -0.5),
        'ada_b': nrm(ks[5], (DEPTH, 6 * D), 0.02),
        'norm_g': 1.0 + nrm(ks[6], (DEPTH, 2, D), 0.02),
        'ab_w_in': nrm(ks[7], (N_EVEN, D, AB_COLS), D ** -0.5),
        'ab_w_out': nrm(ks[8], (N_EVEN, A_W + B_W, D), (A_W + B_W) ** -0.5),
        'na_rpb': nrm(ks[9], (N_EVEN, A_HEADS, 2 * NA_ROWS - 1, 2 * NA_COLS - 1), 0.1),
        'gqa_q_gain': 1.0 + nrm(ks[10], (N_EVEN, HEAD_DIM), 0.02),
        'gqa_k_gain': 1.0 + nrm(ks[11], (N_EVEN, HEAD_DIM), 0.02),
        'cd_w_in': nrm(ks[12], (N_ODD, D, CD_COLS), D ** -0.5),
        'cd_w_out': nrm(ks[13], (N_ODD, C_W + D_VW, D), (C_W + D_VW) ** -0.5),
        'sconv_w': nrm(ks[14], (N_ODD, 3, C_W), 0.5),
        'diff_lq1': nrm(ks[15], (N_ODD, HEAD_DIM), 0.1),
        'diff_lk1': nrm(ks[16], (N_ODD, HEAD_DIM), 0.1),
        'diff_lq2': nrm(ks[17], (N_ODD, HEAD_DIM), 0.1),
        'diff_lk2': nrm(ks[18], (N_ODD, HEAD_DIM), 0.1),
        'diff_subln_g': 1.0 + nrm(ks[19], (N_ODD, 2 * HEAD_DIM), 0.02),
        'ffn_w_gate': nrm(ks[20], (DEPTH, D, D_FF), D ** -0.5),
        'ffn_w_up': nrm(ks[21], (DEPTH, D, D_FF), D ** -0.5),
        'ffn_conv_w': nrm(ks[22], (DEPTH, 3, D_FF), 0.5),
        'ffn_conv_b': nrm(ks[23], (DEPTH, D_FF), 0.02),
        'ffn_w_down': nrm(ks[24], (DEPTH, D_FF, D), D_FF ** -0.5),
        'final_g': 1.0 + nrm(ks[25], (D,), 0.02),
    }


def reference(x, c, ctx, c_ctx, ada_w, ada_b, norm_g, ab_w_in, ab_w_out, na_rpb, gqa_q_gain, gqa_k_gain,
              cd_w_in, cd_w_out, sconv_w, diff_lq1, diff_lk1, diff_lq2, diff_lk2, diff_subln_g,
              ffn_w_gate, ffn_w_up, ffn_conv_w, ffn_conv_b, ffn_w_down, final_g):
    cos, sin = axial_rope_tables(x.shape[1])
    h_ctx = ctx
    for layer in range(DEPTH):
        last = layer == DEPTH - 1
        i = layer // 2
        mod_lat = (jax.nn.silu(c) @ ada_w[layer] + ada_b[layer])[:, None, :]
        mod_ctx = (jax.nn.silu(c_ctx) @ ada_w[layer] + ada_b[layer])[None, None, :]
        sh1, sc1, g1, sh2, sc2, g2 = jnp.split(mod_lat, 6, axis=-1)
        csh1, csc1, cg1, csh2, csc2, cg2 = jnp.split(mod_ctx, 6, axis=-1)
        h_lat = modulate(rms_norm(x, norm_g[layer, 0]), sh1, sc1)
        h_c = modulate(rms_norm(h_ctx, norm_g[layer, 0]), csh1, csc1)
        if layer % 2 == 0:
            y_lat, y_ctx = mixer_ab(h_lat, h_c, ab_w_in[i], ab_w_out[i], na_rpb[i], gqa_q_gain[i],
                                    gqa_k_gain[i], cos, sin, not last)
        else:
            lam_init = 0.8 - 0.6 * math.exp(-0.3 * layer)
            lam = (jnp.exp(jnp.sum(diff_lq1[i].astype(F32) * diff_lk1[i].astype(F32)))
                   - jnp.exp(jnp.sum(diff_lq2[i].astype(F32) * diff_lk2[i].astype(F32))) + lam_init)
            y_lat, y_ctx = mixer_cd(h_lat, h_c, cd_w_in[i], cd_w_out[i], sconv_w[i], lam, lam_init,
                                    diff_subln_g[i], cos, sin, not last)
        x = x + g1 * y_lat
        x = x + g2 * conv_glu(modulate(rms_norm(x, norm_g[layer, 1]), sh2, sc2), ffn_w_gate[layer],
                              ffn_w_up[layer], ffn_conv_w[layer], ffn_conv_b[layer], ffn_w_down[layer])
        if not last:
            h_ctx = h_ctx + cg1 * y_ctx
            h_ctx = h_ctx + cg2 * conv_glu(modulate(rms_norm(h_ctx, norm_g[layer, 1]), csh2, csc2),
                                           ffn_w_gate[layer], ffn_w_up[layer], ffn_conv_w[layer],
                                           ffn_conv_b[layer], ffn_w_down[layer])
    return rms_norm(x, final_g)
```

```python
import functools
import math

import numpy as np
import jax
import jax.numpy as jnp
from jax import lax
from jax.experimental import pallas as pl
from jax.experimental.pallas import tpu as pltpu

HEAD_DIM = 128
GRID_W = 64
ROPE_THETA = 10000.0
EPS = 1e-6
F32 = jnp.float32
BF16 = jnp.bfloat16
NEG = -1e30
VMEM_LIMIT_BYTES = 56 * 1024 * 1024
BF16_SUBLANES = 16
F32_SUBLANES = 8
NT_DIMS = (((1,), (1,)), ((), ()))
FLASH_TQ = 512
FLASH_TK = 1280


def _params(*sem):
    return pltpu.CompilerParams(dimension_semantics=sem, vmem_limit_bytes=VMEM_LIMIT_BYTES)


def _tile(n, want):
    t = min(n, want)
    while n % t:
        t -= 1
    return t


def _norm_mod(x, g, sh, sc):
    y = x * lax.rsqrt(jnp.mean(x * x, axis=-1, keepdims=True) + EPS)
    return (y * g) * (1.0 + sc) + sh


def _ada_kernel(c_ref, w_ref, b_ref, o_ref):
    c = c_ref[...]
    a = c * jax.nn.sigmoid(c)
    a_hi = a.astype(BF16)
    a_lo = (a - a_hi.astype(F32)).astype(BF16)
    w = w_ref[0]
    w_hi = w.astype(BF16)
    w_lo = (w - w_hi.astype(F32)).astype(BF16)
    acc = jnp.dot(a_hi, w_hi, preferred_element_type=F32)
    acc += jnp.dot(a_lo, w_hi, preferred_element_type=F32)
    acc += jnp.dot(a_hi, w_lo, preferred_element_type=F32)
    o_ref[0] = acc + b_ref[0]


def ada_vectors(cvec, ada_w, ada_b):
    depth, d, n = ada_w.shape
    tn = _tile(n, 512)
    return pl.pallas_call(
        _ada_kernel,
        out_shape=jax.ShapeDtypeStruct((depth, F32_SUBLANES, n), F32),
        grid=(depth, n // tn),
        in_specs=[pl.BlockSpec((F32_SUBLANES, d), lambda l, j: (0, 0)),
                  pl.BlockSpec((1, d, tn), lambda l, j: (l, 0, j)),
                  pl.BlockSpec((1, 1, tn), lambda l, j: (l, 0, j))],
        out_specs=pl.BlockSpec((1, F32_SUBLANES, tn), lambda l, j: (l, 0, j)),
        compiler_params=_params("parallel", "parallel"),
        name="ada_vectors",
    )(cvec, ada_w, ada_b.reshape(depth, 1, n))


def _modmm_kernel(x_ref, g_ref, sh_ref, sc_ref, w_ref, o_ref, h_ref):
    @pl.when(pl.program_id(1) == 0)
    def _():
        h_ref[...] = _norm_mod(x_ref[...], g_ref[...], sh_ref[0], sc_ref[0]).astype(BF16)

    o_ref[...] = jnp.dot(h_ref[...], w_ref[...], preferred_element_type=F32).astype(o_ref.dtype)


def mod_matmul(x, gain, shift, scale, w, rows_per_mod):
    m, d = x.shape
    n = w.shape[1]
    tm = _tile(math.gcd(m, rows_per_mod), 512)
    tn = _tile(n, 512)
    mod_spec = pl.BlockSpec((1, 1, d), lambda i, j: (i * tm // rows_per_mod, 0, 0))
    return pl.pallas_call(
        _modmm_kernel,
        out_shape=jax.ShapeDtypeStruct((m, n), BF16),
        grid=(m // tm, n // tn),
        in_specs=[pl.BlockSpec((tm, d), lambda i, j: (i, 0)),
                  pl.BlockSpec((1, d), lambda i, j: (0, 0)),
                  mod_spec, mod_spec,
                  pl.BlockSpec((d, tn), lambda i, j: (0, j))],
        out_specs=pl.BlockSpec((tm, tn), lambda i, j: (i, j)),
        scratch_shapes=[pltpu.VMEM((tm, d), BF16)],
        compiler_params=_params("parallel", "arbitrary"),
        name="mod_matmul",
    )(x, gain.reshape(1, d), shift, scale, w)


def _prep_kernel(*refs, n_heads, use_gain, use_rope, scale):
    refs = list(refs)
    x_ref = refs.pop(0)
    gain = refs.pop(0)[...] if use_gain else None
    if use_rope:
        cos, sin_a, sin_b = refs.pop(0)[...], refs.pop(0)[...], refs.pop(0)[...]
    o_ref = refs.pop(0)
    for h in range(n_heads):
        sl = slice(h * HEAD_DIM, (h + 1) * HEAD_DIM)
        x = x_ref[:, sl].astype(F32)
        if use_gain:
            x = x * lax.rsqrt(jnp.mean(x * x, axis=-1, keepdims=True) + EPS) * gain
        if use_rope:
            x = (x * cos + pltpu.roll(x, HEAD_DIM - HEAD_DIM // 4, 1) * sin_a
                 + pltpu.roll(x, HEAD_DIM // 4, 1) * sin_b)
        if scale != 1.0:
            x = x * scale
        o_ref[:, sl] = x.astype(o_ref.dtype)


def head_prep(p, col_start, width, gain=None, rope=None, scale=1.0, seq=None):
    m = p.shape[0]
    assert col_start % width == 0
    tm = _tile(seq if rope is not None else m, 512)
    args = [p]
    in_specs = [pl.BlockSpec((tm, width), lambda i: (i, col_start // width))]
    if gain is not None:
        args.append(gain.reshape(1, HEAD_DIM).astype(F32))
        in_specs.append(pl.BlockSpec((1, HEAD_DIM), lambda i: (0, 0)))
    if rope is not None:
        n_seq_tiles = seq // tm
        args.extend(rope)
        in_specs.extend([pl.BlockSpec((tm, HEAD_DIM), lambda i: (i % n_seq_tiles, 0))] * 3)
    kern = functools.partial(_prep_kernel, n_heads=width // HEAD_DIM, use_gain=gain is not None,
                             use_rope=rope is not None, scale=scale)
    return pl.pallas_call(
        kern,
        out_shape=jax.ShapeDtypeStruct((m, width), BF16),
        grid=(m // tm,),
        in_specs=in_specs,
        out_specs=pl.BlockSpec((tm, width), lambda i: (i, 0)),
        compiler_params=_params("parallel"),
        name="head_prep",
    )(*args)


def rope_tables(seq):
    t = jnp.arange(seq)
    axis_dim = HEAD_DIM // 2
    inv = 1.0 / (ROPE_THETA ** (jnp.arange(0, axis_dim, 2, dtype=F32) / axis_dim))
    ang_r = (t // GRID_W).astype(F32)[:, None] * inv
    ang_c = (t % GRID_W).astype(F32)[:, None] * inv
    ang = jnp.concatenate([ang_r, ang_r, ang_c, ang_c], axis=-1)
    cos, sin = jnp.cos(ang), jnp.sin(ang)
    first_half = (np.arange(HEAD_DIM) % (HEAD_DIM // 2)) < HEAD_DIM // 4
    sin_a = jnp.where(first_half[None], -sin, 0.0)
    sin_b = jnp.where(first_half[None], 0.0, sin)
    return cos, sin_a, sin_b


def _na_kernel(q_ref, k0_ref, k1_ref, k2_ref, v0_ref, v1_ref, v2_ref, kc_ref, vc_ref, bias_ref, o_ref,
               *, n_heads, scale, tq):
    k_refs = (k0_ref, k1_ref, k2_ref)
    v_refs = (v0_ref, v1_ref, v2_ref)
    for h in range(n_heads):
        sl = slice(h * HEAD_DIM, (h + 1) * HEAD_DIM)
        q = q_ref[:, sl]
        s = [lax.dot_general(q, k_refs[j][:, sl], NT_DIMS, preferred_element_type=F32) * scale
             + bias_ref[0, h, :, j * tq:(j + 1) * tq] for j in range(3)]
        s.append(lax.dot_general(q, kc_ref[:, sl], NT_DIMS, preferred_element_type=F32) * scale)
        m = functools.reduce(jnp.maximum, [jnp.max(t, axis=1, keepdims=True) for t in s])
        p = [jnp.exp(t - m) for t in s]
        l = functools.reduce(jnp.add, [jnp.sum(t, axis=1, keepdims=True) for t in p])
        vs = [v_refs[j][:, sl] for j in range(3)] + [vc_ref[:, sl]]
        o = functools.reduce(jnp.add, [jnp.dot(pj.astype(BF16), vj, preferred_element_type=F32)
                                       for pj, vj in zip(p, vs)])
        o_ref[:, sl] = (o / l).astype(o_ref.dtype)


def _na_bias_table(rpb, rows, na_rows, na_cols):
    rq_rows = na_rows // 2
    n_blk = rows // rq_rows
    tabs = []
    for qb, kb in ((0, 0), (1, 0), (n_blk - 1, n_blk - 3)):
        rq = np.repeat(qb * rq_rows + np.arange(rq_rows), GRID_W)[:, None]
        cq = np.tile(np.arange(GRID_W), rq_rows)[:, None]
        rk = np.repeat(kb * rq_rows + np.arange(3 * rq_rows), GRID_W)[None, :]
        ck = np.tile(np.arange(GRID_W), 3 * rq_rows)[None, :]
        r0 = np.clip(rq - na_rows // 2, 0, rows - na_rows)
        c0 = np.clip(cq - na_cols // 2, 0, GRID_W - na_cols)
        valid = (rk >= r0) & (rk < r0 + na_rows) & (ck >= c0) & (ck < c0 + na_cols)
        ri = np.clip(rk - rq + na_rows - 1, 0, 2 * na_rows - 2)
        ci = np.clip(ck - cq + na_cols - 1, 0, 2 * na_cols - 2)
        tabs.append(jnp.where(valid[None], rpb[:, ri, ci].astype(F32), NEG))
    return jnp.stack(tabs)


def neighbourhood_attention(p_lat, p_ctx, rpb, batch, seq, ctx_len, a_w, k_col, v_col, ck_col, cv_col):
    n_heads = a_w // HEAD_DIM
    na_rows, na_cols = (rpb.shape[1] + 1) // 2, (rpb.shape[2] + 1) // 2
    rows = seq // GRID_W
    rq_rows = na_rows // 2
    tq = rq_rows * GRID_W
    n_blk = rows // rq_rows
    assert rows >= na_rows and rows % rq_rows == 0 and n_blk >= 3 and na_rows % 2 == 0
    assert k_col % a_w == 0 and v_col % a_w == 0 and ck_col % a_w == 0 and cv_col % a_w == 0
    bias = _na_bias_table(rpb, rows, na_rows, na_cols)

    def kv_spec(col, j):
        return pl.BlockSpec((tq, a_w), lambda b, r: (b * n_blk + jnp.clip(r - 1, 0, n_blk - 3) + j, col // a_w))

    kern = functools.partial(_na_kernel, n_heads=n_heads, scale=HEAD_DIM ** -0.5, tq=tq)
    return pl.pallas_call(
        kern,
        out_shape=jax.ShapeDtypeStruct((batch * seq, a_w), BF16),
        grid=(batch, n_blk),
        in_specs=[pl.BlockSpec((tq, a_w), lambda b, r: (b * n_blk + r, 0)),
                  kv_spec(k_col, 0), kv_spec(k_col, 1), kv_spec(k_col, 2),
                  kv_spec(v_col, 0), kv_spec(v_col, 1), kv_spec(v_col, 2),
                  pl.BlockSpec((ctx_len, a_w), lambda b, r: (b, ck_col // a_w)),
                  pl.BlockSpec((ctx_len, a_w), lambda b, r: (b, cv_col // a_w)),
                  pl.BlockSpec((1, n_heads, tq, 3 * tq),
                               lambda b, r: (jnp.where(r == 0, 0, jnp.where(r == n_blk - 1, 2, 1)), 0, 0, 0))],
        out_specs=pl.BlockSpec((tq, a_w), lambda b, r: (b * n_blk + r, 0)),
        compiler_params=_params("parallel", "arbitrary"),
        name="neighbourhood_attention",
    )(p_lat, p_lat, p_lat, p_lat, p_lat, p_lat, p_lat, p_ctx, p_ctx, bias)


def _flash_kernel(q_ref, k_ref, v_ref, o_ref, m_ref, l_ref, acc_ref, *, group):
    kv = pl.program_id(3)

    @pl.when(kv == 0)
    def _():
        m_ref[...] = jnp.full(m_ref.shape, NEG, F32)
        l_ref[...] = jnp.zeros(l_ref.shape, F32)
        acc_ref[...] = jnp.zeros(acc_ref.shape, F32)

    k = k_ref[0]
    v = v_ref[0]
    for r in range(group):
        q = q_ref[:, r * HEAD_DIM:(r + 1) * HEAD_DIM]
        s = lax.dot_general(q, k, NT_DIMS, preferred_element_type=F32)
        m_prev = m_ref[r]
        m_new = jnp.maximum(m_prev, jnp.max(s, axis=1, keepdims=True))
        alpha = jnp.exp(m_prev - m_new)
        p = jnp.exp(s - m_new[:, :1])
        l_ref[r] = alpha * l_ref[r] + jnp.sum(p, axis=1, keepdims=True)
        acc_ref[r] = alpha * acc_ref[r] + jnp.dot(p.astype(BF16), v, preferred_element_type=F32)
        m_ref[r] = m_new

    @pl.when(kv == pl.num_programs(3) - 1)
    def _():
        for r in range(group):
            o_ref[:, r * HEAD_DIM:(r + 1) * HEAD_DIM] = (acc_ref[r] / l_ref[r]).astype(o_ref.dtype)


def flash_gqa(q, k, v, batch, n_kv_heads, group, k_col=0, v_col=0):
    sq = q.shape[0] // batch
    sk = k.shape[1]
    tq = _tile(sq, FLASH_TQ)
    tk = _tile(sk, FLASH_TK)
    assert tk % BF16_SUBLANES == 0 and k_col % HEAD_DIM == 0 and v_col % HEAD_DIM == 0
    nq = sq // tq
    kern = functools.partial(_flash_kernel, group=group)
    return pl.pallas_call(
        kern,
        out_shape=jax.ShapeDtypeStruct(q.shape, BF16),
        grid=(batch, n_kv_heads, nq, sk // tk),
        in_specs=[pl.BlockSpec((tq, group * HEAD_DIM), lambda b, g, i, j: (b * nq + i, g)),
                  pl.BlockSpec((1, tk, HEAD_DIM), lambda b, g, i, j: (b, j, k_col // HEAD_DIM + g)),
                  pl.BlockSpec((1, tk, HEAD_DIM), lambda b, g, i, j: (b, j, v_col // HEAD_DIM + g))],
        out_specs=pl.BlockSpec((tq, group * HEAD_DIM), lambda b, g, i, j: (b * nq + i, g)),
        scratch_shapes=[pltpu.VMEM((group, tq, HEAD_DIM), F32),
                        pltpu.VMEM((group, tq, HEAD_DIM), F32),
                        pltpu.VMEM((group, tq, HEAD_DIM), F32)],
        compiler_params=_params("parallel", "parallel", "parallel", "arbitrary"),
        name="flash_gqa",
    )(q, k, v)


def _diff_kernel(q_ref, k_ref, v_ref, lq1_ref, lk1_ref, lq2_ref, lk2_ref, g_ref, o_ref,
                 m_ref, l_ref, acc_ref, *, lam_init):
    kv = pl.program_id(3)

    @pl.when(kv == 0)
    def _():
        m_ref[...] = jnp.full(m_ref.shape, NEG, F32)
        l_ref[...] = jnp.zeros(l_ref.shape, F32)
        acc_ref[...] = jnp.zeros(acc_ref.shape, F32)

    v = v_ref[0]
    for c in range(2):
        sl = slice(c * HEAD_DIM, (c + 1) * HEAD_DIM)
        s = lax.dot_general(q_ref[:, sl], k_ref[0, :, sl], NT_DIMS, preferred_element_type=F32)
        m_prev = m_ref[c]
        m_new = jnp.maximum(m_prev, jnp.max(s, axis=1, keepdims=True))
        alpha = jnp.exp(m_prev - m_new)
        p = jnp.exp(s - m_new[:, :1])
        l_ref[c] = alpha * l_ref[c] + jnp.sum(p, axis=1, keepdims=True)
        acc_ref[c] = alpha[:, :1] * acc_ref[c] + jnp.dot(p.astype(BF16), v, preferred_element_type=F32)
        m_ref[c] = m_new

    @pl.when(kv == pl.num_programs(3) - 1)
    def _():
        lam = (jnp.exp(jnp.sum(lq1_ref[...] * lk1_ref[...], axis=-1, keepdims=True))
               - jnp.exp(jnp.sum(lq2_ref[...] * lk2_ref[...], axis=-1, keepdims=True)) + lam_init)
        o = acc_ref[0] / l_ref[0][:, :1] - lam * (acc_ref[1] / l_ref[1][:, :1])
        o = o * lax.rsqrt(jnp.mean(o * o, axis=-1, keepdims=True) + EPS) * g_ref[...]
        o_ref[...] = (o * (1.0 - lam_init)).astype(o_ref.dtype)


def flash_diff(q, k, v, lam_vecs, subln_g, lam_init, batch, n_heads):
    sq = q.shape[0] // batch
    sk = k.shape[1]
    tq = _tile(sq, FLASH_TQ)
    tk = _tile(sk, FLASH_TK)
    assert tk % BF16_SUBLANES == 0
    nq = sq // tq
    w = 2 * HEAD_DIM
    vec_spec = pl.BlockSpec((1, HEAD_DIM), lambda b, h, i, j: (0, 0))
    kern = functools.partial(_diff_kernel, lam_init=lam_init)
    return pl.pallas_call(
        kern,
        out_shape=jax.ShapeDtypeStruct(q.shape, BF16),
        grid=(batch, n_heads, nq, sk // tk),
        in_specs=[pl.BlockSpec((tq, w), lambda b, h, i, j: (b * nq + i, h)),
                  pl.BlockSpec((1, tk, w), lambda b, h, i, j: (b, j, h)),
                  pl.BlockSpec((1, tk, w), lambda b, h, i, j: (b, j, h)),
                  vec_spec, vec_spec, vec_spec, vec_spec,
                  pl.BlockSpec((1, w), lambda b, h, i, j: (0, 0))],
        out_specs=pl.BlockSpec((tq, w), lambda b, h, i, j: (b * nq + i, h)),
        scratch_shapes=[pltpu.VMEM((2, tq, HEAD_DIM), F32),
                        pltpu.VMEM((2, tq, HEAD_DIM), F32),
                        pltpu.VMEM((2, tq, w), F32)],
        compiler_params=_params("parallel", "parallel", "parallel", "arbitrary"),
        name="flash_diff",
    )(q, k, v, *[t.reshape(1, HEAD_DIM).astype(F32) for t in lam_vecs], subln_g.reshape(1, w).astype(F32))


def _shift_rows(x, first_row, last_row):
    n = x.shape[0]
    row = lax.broadcasted_iota(jnp.int32, x.shape, 0)
    up = jnp.where(row == 0, first_row, pltpu.roll(x, 1, 0))
    dn = jnp.where(row == n - 1, last_row, pltpu.roll(x, n - 1, 0))
    return up, dn


def _sconv_kernel(u_ref, gb_ref, gc_ref, up_ref, gcp_ref, un_ref, gcn_ref, w_ref, o_ref, *, tiles_per_seq):
    i = pl.program_id(0) % tiles_per_seq
    has_prev = (i != 0).astype(F32)
    has_next = (i != tiles_per_seq - 1).astype(F32)
    x = gc_ref[...].astype(F32) * u_ref[...].astype(F32)
    prev = (gcp_ref[...].astype(F32) * up_ref[...].astype(F32))[BF16_SUBLANES - 1:, :] * has_prev
    nxt = (gcn_ref[...].astype(F32) * un_ref[...].astype(F32))[:1, :] * has_next
    x_up, x_dn = _shift_rows(x, prev, nxt)
    w = w_ref[...]
    y = x_up * w[0:1] + x * w[1:2] + x_dn * w[2:3]
    o_ref[...] = (gb_ref[...].astype(F32) * y).astype(o_ref.dtype)


def short_conv(p, conv_w, seq):
    m = p.shape[0]
    c = conv_w.shape[1]
    tm = _tile(seq, 512)
    hb = BF16_SUBLANES
    n_halo = m // hb
    kern = functools.partial(_sconv_kernel, tiles_per_seq=seq // tm)

    def prev_spec(col):
        return pl.BlockSpec((hb, c), lambda i: (jnp.maximum(i * (tm // hb) - 1, 0), col))

    def next_spec(col):
        return pl.BlockSpec((hb, c), lambda i: (jnp.minimum((i + 1) * (tm // hb), n_halo - 1), col))

    return pl.pallas_call(
        kern,
        out_shape=jax.ShapeDtypeStruct((m, c), BF16),
        grid=(m // tm,),
        in_specs=[pl.BlockSpec((tm, c), lambda i: (i, 0)),
                  pl.BlockSpec((tm, c), lambda i: (i, 1)),
                  pl.BlockSpec((tm, c), lambda i: (i, 2)),
                  prev_spec(0), prev_spec(2), next_spec(0), next_spec(2),
                  pl.BlockSpec((3, c), lambda i: (0, 0))],
        out_specs=pl.BlockSpec((tm, c), lambda i: (i, 0)),
        compiler_params=_params("parallel"),
        name="short_conv",
    )(p, p, p, p, p, p, p, conv_w.astype(F32))


def _outproj_kernel(a_ref, b_ref, wa_ref, wb_ref, x_ref, g_ref, o_ref):
    y = jnp.dot(a_ref[...], wa_ref[...], preferred_element_type=F32)
    y += jnp.dot(b_ref[...], wb_ref[...], preferred_element_type=F32)
    o_ref[...] = x_ref[...] + g_ref[0] * y


def out_proj_residual(x, a, b, w, gate, rows_per_mod):
    m, d = x.shape
    ka, kb = a.shape[1], b.shape[1]
    tm = _tile(math.gcd(m, rows_per_mod), 512)
    tn = _tile(d, 512)
    return pl.pallas_call(
        _outproj_kernel,
        out_shape=jax.ShapeDtypeStruct((m, d), F32),
        grid=(m // tm, d // tn),
        in_specs=[pl.BlockSpec((tm, ka), lambda i, j: (i, 0)),
                  pl.BlockSpec((tm, kb), lambda i, j: (i, 0)),
                  pl.BlockSpec((ka, tn), lambda i, j: (0, j)),
                  pl.BlockSpec((kb, tn), lambda i, j: (0, j)),
                  pl.BlockSpec((tm, tn), lambda i, j: (i, j)),
                  pl.BlockSpec((1, 1, tn), lambda i, j: (i * tm // rows_per_mod, 0, j))],
        out_specs=pl.BlockSpec((tm, tn), lambda i, j: (i, j)),
        compiler_params=_params("parallel", "parallel"),
        name="out_proj_residual",
    )(a, b, w[:ka], w[ka:], x, gate)


def _ffn_kernel(x_ref, xp_ref, xn_ref, g_ref, sh_ref, sc_ref, gate_ref, wg_ref, wu_ref, cw_ref, cb_ref, wd_ref,
                *rest, tiles_per_seq, final_norm):
    if final_norm:
        fg_ref, o_ref, h_ref, hp_ref, hn_ref, acc_ref = rest
    else:
        o_ref, h_ref, hp_ref, hn_ref, acc_ref = rest
    j = pl.program_id(1)
    i = pl.program_id(0) % tiles_per_seq

    @pl.when(j == 0)
    def _():
        g, sh, sc = g_ref[...], sh_ref[0], sc_ref[0]
        h_ref[...] = _norm_mod(x_ref[...], g, sh, sc).astype(BF16)
        hp_ref[...] = _norm_mod(xp_ref[...], g, sh, sc).astype(BF16)
        hn_ref[...] = _norm_mod(xn_ref[...], g, sh, sc).astype(BF16)
        acc_ref[...] = jnp.zeros(acc_ref.shape, F32)

    wg = wg_ref[...]
    gm = jnp.dot(h_ref[...], wg, preferred_element_type=F32)
    has_prev = (i != 0).astype(F32)
    has_next = (i != tiles_per_seq - 1).astype(F32)
    g_prev = jnp.dot(hp_ref[...], wg, preferred_element_type=F32)[F32_SUBLANES - 1:, :] * has_prev
    g_next = jnp.dot(hn_ref[...], wg, preferred_element_type=F32)[:1, :] * has_next
    g_up, g_dn = _shift_rows(gm, g_prev, g_next)
    cw = cw_ref[...]
    conv = g_up * cw[0:1] + gm * cw[1:2] + g_dn * cw[2:3] + cb_ref[...]
    up = jnp.dot(h_ref[...], wu_ref[...], preferred_element_type=F32)
    act = (conv * jax.nn.sigmoid(conv) * up).astype(BF16)
    acc_ref[...] += jnp.dot(act, wd_ref[...], preferred_element_type=F32)

    @pl.when(j == pl.num_programs(1) - 1)
    def _():
        y = x_ref[...] + gate_ref[0] * acc_ref[...]
        if final_norm:
            y = y * lax.rsqrt(jnp.mean(y * y, axis=-1, keepdims=True) + EPS) * fg_ref[...]
        o_ref[...] = y


def conv_glu_residual(x, gain, shift, scale, gate, w_gate, w_up, conv_w, conv_b, w_down, seq, rows_per_mod,
                      final_gain=None):
    m, d = x.shape
    f = w_gate.shape[1]
    tm = _tile(math.gcd(seq, rows_per_mod), 512)
    tf = _tile(f, 512)
    hb = F32_SUBLANES
    n_halo = m // hb
    mod_spec = pl.BlockSpec((1, 1, d), lambda i, j: (i * tm // rows_per_mod, 0, 0))
    row_spec = pl.BlockSpec((1, d), lambda i, j: (0, 0))
    in_specs = [pl.BlockSpec((tm, d), lambda i, j: (i, 0)),
                pl.BlockSpec((hb, d), lambda i, j: (jnp.maximum(i * (tm // hb) - 1, 0), 0)),
                pl.BlockSpec((hb, d), lambda i, j: (jnp.minimum((i + 1) * (tm // hb), n_halo - 1), 0)),
                row_spec, mod_spec, mod_spec, mod_spec,
                pl.BlockSpec((d, tf), lambda i, j: (0, j)),
                pl.BlockSpec((d, tf), lambda i, j: (0, j)),
                pl.BlockSpec((3, tf), lambda i, j: (0, j)),
                pl.BlockSpec((1, tf), lambda i, j: (0, j)),
                pl.BlockSpec((tf, d), lambda i, j: (j, 0))]
    args = [x, x, x, gain.reshape(1, d), shift, scale, gate, w_gate, w_up,
            conv_w.astype(F32), conv_b.reshape(1, f).astype(F32), w_down]
    if final_gain is not None:
        in_specs.append(row_spec)
        args.append(final_gain.reshape(1, d).astype(F32))
    kern = functools.partial(_ffn_kernel, tiles_per_seq=seq // tm, final_norm=final_gain is not None)
    return pl.pallas_call(
        kern,
        out_shape=jax.ShapeDtypeStruct((m, d), F32),
        grid=(m // tm, f // tf),
        in_specs=in_specs,
        out_specs=pl.BlockSpec((tm, d), lambda i, j: (i, 0)),
        scratch_shapes=[pltpu.VMEM((tm, d), BF16), pltpu.VMEM((hb, d), BF16), pltpu.VMEM((hb, d), BF16),
                        pltpu.VMEM((tm, d), F32)],
        compiler_params=_params("parallel", "arbitrary"),
        name="conv_glu_residual",
    )(*args)


def kernel(x, c, ctx, c_ctx, ada_w, ada_b, norm_g, ab_w_in, ab_w_out, na_rpb, gqa_q_gain, gqa_k_gain, cd_w_in,
           cd_w_out, sconv_w, diff_lq1, diff_lk1, diff_lq2, diff_lk2, diff_subln_g, ffn_w_gate, ffn_w_up,
           ffn_conv_w, ffn_conv_b, ffn_w_down, final_g):
    batch, seq, d = x.shape
    ctx_len = ctx.shape[1]
    depth = ada_w.shape[0]
    assert depth == 2 and batch < F32_SUBLANES
    a_w = na_rpb.shape[1] * HEAD_DIM
    b_w = ab_w_out.shape[1] - a_w
    b_kv_w = (ab_w_in.shape[2] - 3 * a_w - b_w) // 2
    n_kv = b_kv_w // HEAD_DIM
    group = b_w // b_kv_w
    ab_kv = a_w + b_w
    c_w = sconv_w.shape[2]
    d_vw = cd_w_out.shape[1] - c_w
    d_heads = d_vw // (2 * HEAD_DIM)
    cd_kv = 3 * c_w + d_vw
    assert c_w == d_vw and cd_w_in.shape[2] == cd_kv + 2 * d_vw
    scale = HEAD_DIM ** -0.5
    ml, mc = batch * seq, batch * ctx_len

    cvec = jnp.zeros((F32_SUBLANES, d), F32).at[:batch].set(c).at[batch].set(c_ctx)
    mods = ada_vectors(cvec, ada_w, ada_b).reshape(depth, F32_SUBLANES, 6, 1, d)
    rope = rope_tables(seq)
    x = x.reshape(ml, d)
    h_ctx = ctx.reshape(mc, d)

    for layer in range(depth):
        last = layer == depth - 1
        i = layer // 2
        lat = [mods[layer, :batch, t] for t in range(6)]
        cm = [mods[layer, batch:batch + 1, t] for t in range(6)]
        if layer % 2 == 0:
            w_in = ab_w_in[i].astype(BF16)
            p = mod_matmul(x, norm_g[layer, 0], lat[0], lat[1], w_in, seq)
            pc = mod_matmul(h_ctx, norm_g[layer, 0], cm[0], cm[1], w_in, mc)
            bq = head_prep(p, a_w, b_w, gain=gqa_q_gain[i], rope=rope, scale=scale, seq=seq)
            bk = head_prep(p, ab_kv + 2 * a_w, b_kv_w, gain=gqa_k_gain[i], rope=rope, seq=seq)
            bck = head_prep(pc, ab_kv + 2 * a_w, b_kv_w, gain=gqa_k_gain[i])
            v_col = ab_kv + 2 * a_w + b_kv_w
            k_all = jnp.concatenate([bck.reshape(batch, ctx_len, b_kv_w), bk.reshape(batch, seq, b_kv_w)], axis=1)
            v_all = jnp.concatenate([pc[:, v_col:].reshape(batch, ctx_len, b_kv_w),
                                     p[:, v_col:].reshape(batch, seq, b_kv_w)], axis=1)
            o_a = neighbourhood_attention(p, pc, na_rpb[i], batch, seq, ctx_len, a_w,
                                          ab_kv, ab_kv + a_w, ab_kv, ab_kv + a_w)
            o_b = flash_gqa(bq, k_all, v_all, batch, n_kv, group)
            w_out = ab_w_out[i].astype(BF16)
            x = out_proj_residual(x, o_a, o_b, w_out, lat[2], seq)
            if not last:
                acq = head_prep(pc, 0, a_w, scale=scale)
                bcq = head_prep(pc, a_w, b_w, gain=gqa_q_gain[i], scale=scale)
                pc3 = pc.reshape(batch, ctx_len, pc.shape[1])
                oc_a = flash_gqa(acq, pc3, pc3, batch, a_w // HEAD_DIM, 1, k_col=ab_kv, v_col=ab_kv + a_w)
                oc_b = flash_gqa(bcq, bck.reshape(batch, ctx_len, b_kv_w), pc3, batch, n_kv, group, v_col=v_col)
                h_ctx = out_proj_residual(h_ctx, oc_a, oc_b, w_out, cm[2], mc)
        else:
            assert last, "a non-final short-conv / differential layer is not needed at this depth"
            lam_init = 0.8 - 0.6 * math.exp(-0.3 * layer)
            w_in = cd_w_in[i].astype(BF16)
            p = mod_matmul(x, norm_g[layer, 0], lat[0], lat[1], w_in, seq)
            pc = mod_matmul(h_ctx, norm_g[layer, 0], cm[0], cm[1], w_in[:, cd_kv:], mc)
            dq = head_prep(p, 3 * c_w, d_vw, rope=rope, scale=scale, seq=seq)
            dk = head_prep(p, cd_kv, d_vw, rope=rope, seq=seq)
            k_all = jnp.concatenate([pc[:, :d_vw].reshape(batch, ctx_len, d_vw), dk.reshape(batch, seq, d_vw)], axis=1)
            v_all = jnp.concatenate([pc[:, d_vw:].reshape(batch, ctx_len, d_vw),
                                     p[:, cd_kv + d_vw:].reshape(batch, seq, d_vw)], axis=1)
            y_c = short_conv(p, sconv_w[i], seq)
            o_d = flash_diff(dq, k_all, v_all, (diff_lq1[i], diff_lk1[i], diff_lq2[i], diff_lk2[i]),
                             diff_subln_g[i], lam_init, batch, d_heads)
            x = out_proj_residual(x, y_c, o_d, cd_w_out[i].astype(BF16), lat[2], seq)
        ffn_w = (ffn_w_gate[layer].astype(BF16), ffn_w_up[layer].astype(BF16), ffn_conv_w[layer],
                 ffn_conv_b[layer], ffn_w_down[layer].astype(BF16))
        x = conv_glu_residual(x, norm_g[layer, 1], lat[3], lat[4], lat[5], *ffn_w, seq, seq,
                              final_gain=final_g if last else None)
        if not last:
            h_ctx = conv_glu_residual(h_ctx, norm_g[layer, 1], cm[3], cm[4], cm[5], *ffn_w, ctx_len, mc)
    return x.reshape(batch, seq, d)
```

```python
import functools
import math

import numpy as np
import jax
import jax.numpy as jnp
from jax import lax
from jax.experimental import pallas as pl
from jax.experimental.pallas import tpu as pltpu

HEAD_DIM = 128
GRID_W = 64
ROPE_THETA = 10000.0
EPS = 1e-6
F32 = jnp.float32
BF16 = jnp.bfloat16
NEG = -1e30
VMEM_LIMIT_BYTES = 56 * 1024 * 1024
BF16_SUBLANES = 16
F32_SUBLANES = 8
NT_DIMS = (((1,), (1,)), ((), ()))
FLASH_TQ = 1024
FLASH_TK = 3328
FLASH_ITEM_ROWS = 256
LOG2E = math.log2(math.e)


def _params(*sem):
    return pltpu.CompilerParams(dimension_semantics=sem, vmem_limit_bytes=VMEM_LIMIT_BYTES)


def _tile(n, want):
    t = min(n, want)
    while n % t:
        t -= 1
    return t


def _norm_mod(x, g, sh, sc):
    y = x * lax.rsqrt(jnp.mean(x * x, axis=-1, keepdims=True) + EPS)
    return (y * g) * (1.0 + sc) + sh


def _ada_kernel(c_ref, w_ref, b_ref, o_ref):
    c = c_ref[...]
    a = c * jax.nn.sigmoid(c)
    a_hi = a.astype(BF16)
    a_lo = (a - a_hi.astype(F32)).astype(BF16)
    w = w_ref[0]
    w_hi = w.astype(BF16)
    w_lo = (w - w_hi.astype(F32)).astype(BF16)
    acc = jnp.dot(a_hi, w_hi, preferred_element_type=F32)
    acc += jnp.dot(a_lo, w_hi, preferred_element_type=F32)
    acc += jnp.dot(a_hi, w_lo, preferred_element_type=F32)
    o_ref[0] = acc + b_ref[0]


def ada_vectors(cvec, ada_w, ada_b):
    depth, d, n = ada_w.shape
    tn = _tile(n, 512)
    return pl.pallas_call(
        _ada_kernel,
        out_shape=jax.ShapeDtypeStruct((depth, F32_SUBLANES, n), F32),
        grid=(depth, n // tn),
        in_specs=[pl.BlockSpec((F32_SUBLANES, d), lambda l, j: (0, 0)),
                  pl.BlockSpec((1, d, tn), lambda l, j: (l, 0, j)),
                  pl.BlockSpec((1, 1, tn), lambda l, j: (l, 0, j))],
        out_specs=pl.BlockSpec((1, F32_SUBLANES, tn), lambda l, j: (l, 0, j)),
        compiler_params=_params("parallel", "parallel"),
        name="ada_vectors",
    )(cvec, ada_w, ada_b.reshape(depth, 1, n))


def _modmm_kernel(x_ref, g_ref, sh_ref, sc_ref, w_ref, o_ref, h_ref):
    @pl.when(pl.program_id(1) == 0)
    def _():
        h_ref[...] = _norm_mod(x_ref[...], g_ref[...], sh_ref[0], sc_ref[0]).astype(BF16)

    o_ref[...] = jnp.dot(h_ref[...], w_ref[...], preferred_element_type=F32).astype(o_ref.dtype)


def mod_matmul(x, gain, shift, scale, w, rows_per_mod):
    m, d = x.shape
    n = w.shape[1]
    tm = _tile(math.gcd(m, rows_per_mod), 1024)
    tn = _tile(n, 512)
    mod_spec = pl.BlockSpec((1, 1, d), lambda i, j: (i * tm // rows_per_mod, 0, 0))
    return pl.pallas_call(
        _modmm_kernel,
        out_shape=jax.ShapeDtypeStruct((m, n), BF16),
        grid=(m // tm, n // tn),
        in_specs=[pl.BlockSpec((tm, d), lambda i, j: (i, 0)),
                  pl.BlockSpec((1, d), lambda i, j: (0, 0)),
                  mod_spec, mod_spec,
                  pl.BlockSpec((d, tn), lambda i, j: (0, j))],
        out_specs=pl.BlockSpec((tm, tn), lambda i, j: (i, j)),
        scratch_shapes=[pltpu.VMEM((tm, d), BF16)],
        compiler_params=_params("parallel", "arbitrary"),
        name="mod_matmul",
    )(x, gain.reshape(1, d), shift, scale, w)


def _prep_kernel(*refs, n_heads, use_gain, use_rope, scale):
    refs = list(refs)
    x_ref = refs.pop(0)
    gain = refs.pop(0)[...] if use_gain else None
    if use_rope:
        cos, sin_a, sin_b = refs.pop(0)[...], refs.pop(0)[...], refs.pop(0)[...]
    o_ref = refs.pop(0)
    for h in range(n_heads):
        sl = slice(h * HEAD_DIM, (h + 1) * HEAD_DIM)
        x = x_ref[:, sl].astype(F32)
        if use_gain:
            x = x * lax.rsqrt(jnp.mean(x * x, axis=-1, keepdims=True) + EPS) * gain
        if use_rope:
            x = (x * cos + pltpu.roll(x, HEAD_DIM - HEAD_DIM // 4, 1) * sin_a
                 + pltpu.roll(x, HEAD_DIM // 4, 1) * sin_b)
        if scale != 1.0:
            x = x * scale
        o_ref[:, sl] = x.astype(o_ref.dtype)


def head_prep(p, col_start, width, gain=None, rope=None, scale=1.0, seq=None):
    m = p.shape[0]
    assert col_start % width == 0
    tm = _tile(seq if rope is not None else m, 512)
    args = [p]
    in_specs = [pl.BlockSpec((tm, width), lambda i: (i, col_start // width))]
    if gain is not None:
        args.append(gain.reshape(1, HEAD_DIM).astype(F32))
        in_specs.append(pl.BlockSpec((1, HEAD_DIM), lambda i: (0, 0)))
    if rope is not None:
        n_seq_tiles = seq // tm
        args.extend(rope)
        in_specs.extend([pl.BlockSpec((tm, HEAD_DIM), lambda i: (i % n_seq_tiles, 0))] * 3)
    kern = functools.partial(_prep_kernel, n_heads=width // HEAD_DIM, use_gain=gain is not None,
                             use_rope=rope is not None, scale=scale)
    return pl.pallas_call(
        kern,
        out_shape=jax.ShapeDtypeStruct((m, width), BF16),
        grid=(m // tm,),
        in_specs=in_specs,
        out_specs=pl.BlockSpec((tm, width), lambda i: (i, 0)),
        compiler_params=_params("parallel"),
        name="head_prep",
    )(*args)


def rope_tables(seq):
    t = jnp.arange(seq)
    axis_dim = HEAD_DIM // 2
    inv = 1.0 / (ROPE_THETA ** (jnp.arange(0, axis_dim, 2, dtype=F32) / axis_dim))
    ang_r = (t // GRID_W).astype(F32)[:, None] * inv
    ang_c = (t % GRID_W).astype(F32)[:, None] * inv
    ang = jnp.concatenate([ang_r, ang_r, ang_c, ang_c], axis=-1)
    cos, sin = jnp.cos(ang), jnp.sin(ang)
    first_half = (np.arange(HEAD_DIM) % (HEAD_DIM // 2)) < HEAD_DIM // 4
    sin_a = jnp.where(first_half[None], -sin, 0.0)
    sin_b = jnp.where(first_half[None], 0.0, sin)
    return cos, sin_a, sin_b


def _na_kernel(q_ref, k0_ref, k1_ref, k2_ref, v0_ref, v1_ref, v2_ref, kc_ref, vc_ref, bias_ref, o_ref,
               *, n_heads, scale, tq):
    k_refs = (k0_ref, k1_ref, k2_ref)
    v_refs = (v0_ref, v1_ref, v2_ref)
    for h in range(n_heads):
        sl = slice(h * HEAD_DIM, (h + 1) * HEAD_DIM)
        q = q_ref[:, sl]
        s = [lax.dot_general(q, k_refs[j][:, sl], NT_DIMS, preferred_element_type=F32) * scale
             + bias_ref[0, h, :, j * tq:(j + 1) * tq] for j in range(3)]
        s.append(lax.dot_general(q, kc_ref[:, sl], NT_DIMS, preferred_element_type=F32) * scale)
        m = functools.reduce(jnp.maximum, [jnp.max(t, axis=1, keepdims=True) for t in s])
        p = [jnp.exp(t - m) for t in s]
        l = functools.reduce(jnp.add, [jnp.sum(t, axis=1, keepdims=True) for t in p])
        vs = [v_refs[j][:, sl] for j in range(3)] + [vc_ref[:, sl]]
        o = functools.reduce(jnp.add, [jnp.dot(pj.astype(BF16), vj, preferred_element_type=F32)
                                       for pj, vj in zip(p, vs)])
        o_ref[:, sl] = (o / l).astype(o_ref.dtype)


def _na_bias_table(rpb, rows, na_rows, na_cols):
    rq_rows = na_rows // 2
    n_blk = rows // rq_rows
    n_heads = rpb.shape[0]
    w = GRID_W
    assert na_cols <= w
    v = jnp.concatenate([rpb[:, :, na_cols - 1:], jnp.zeros((n_heads, rpb.shape[1], 2 * w - rpb.shape[2]), rpb.dtype),
                         rpb[:, :, :na_cols - 1]], axis=-1).astype(F32)
    toep = jnp.tile(v, (1, 1, w + 1))[:, :, :w * (2 * w - 1)].reshape(n_heads, rpb.shape[1], w, 2 * w - 1)[..., :w]
    tabs = []
    for qb, kb in ((0, 0), (1, 0), (n_blk - 1, n_blk - 3)):
        rq = np.repeat(qb * rq_rows + np.arange(rq_rows), w)[:, None]
        cq = np.tile(np.arange(w), rq_rows)[:, None]
        rk = np.repeat(kb * rq_rows + np.arange(3 * rq_rows), w)[None, :]
        ck = np.tile(np.arange(w), 3 * rq_rows)[None, :]
        r0 = np.clip(rq - na_rows // 2, 0, rows - na_rows)
        c0 = np.clip(cq - na_cols // 2, 0, w - na_cols)
        valid = (rk >= r0) & (rk < r0 + na_rows) & (ck >= c0) & (ck < c0 + na_cols)
        row_blocks = []
        for i in range(rq_rows):
            dr = [int(np.clip((kb - qb) * rq_rows + j - i + na_rows - 1, 0, 2 * na_rows - 2)) for j in range(3 * rq_rows)]
            row_blocks.append(jnp.concatenate([toep[:, d] for d in dr], axis=-1))
        tabs.append(jnp.where(valid[None], jnp.concatenate(row_blocks, axis=1), NEG))
    return jnp.stack(tabs)


def neighbourhood_attention(p_lat, p_ctx, rpb, batch, seq, ctx_len, a_w, k_col, v_col, ck_col, cv_col):
    n_heads = a_w // HEAD_DIM
    na_rows, na_cols = (rpb.shape[1] + 1) // 2, (rpb.shape[2] + 1) // 2
    rows = seq // GRID_W
    rq_rows = na_rows // 2
    tq = rq_rows * GRID_W
    n_blk = rows // rq_rows
    assert rows >= na_rows and rows % rq_rows == 0 and n_blk >= 3 and na_rows % 2 == 0
    assert k_col % a_w == 0 and v_col % a_w == 0 and ck_col % a_w == 0 and cv_col % a_w == 0
    bias = _na_bias_table(rpb, rows, na_rows, na_cols)

    def kv_spec(col, j):
        return pl.BlockSpec((tq, a_w), lambda b, r: (b * n_blk + jnp.clip(r - 1, 0, n_blk - 3) + j, col // a_w))

    kern = functools.partial(_na_kernel, n_heads=n_heads, scale=HEAD_DIM ** -0.5, tq=tq)
    return pl.pallas_call(
        kern,
        out_shape=jax.ShapeDtypeStruct((batch * seq, a_w), BF16),
        grid=(batch, n_blk),
        in_specs=[pl.BlockSpec((tq, a_w), lambda b, r: (b * n_blk + r, 0)),
                  kv_spec(k_col, 0), kv_spec(k_col, 1), kv_spec(k_col, 2),
                  kv_spec(v_col, 0), kv_spec(v_col, 1), kv_spec(v_col, 2),
                  pl.BlockSpec((ctx_len, a_w), lambda b, r: (b, ck_col // a_w)),
                  pl.BlockSpec((ctx_len, a_w), lambda b, r: (b, cv_col // a_w)),
                  pl.BlockSpec((1, n_heads, tq, 3 * tq),
                               lambda b, r: (jnp.where(r == 0, 0, jnp.where(r == n_blk - 1, 2, 1)), 0, 0, 0))],
        out_specs=pl.BlockSpec((tq, a_w), lambda b, r: (b * n_blk + r, 0)),
        compiler_params=_params("parallel", "arbitrary"),
        name="neighbourhood_attention",
    )(p_lat, p_lat, p_lat, p_lat, p_lat, p_lat, p_lat, p_ctx, p_ctx, bias)


def _pipelined(n_items, scores, softmax, accumulate):
    ahead, lag = 2, 1
    s = {j: scores(j) for j in range(min(ahead, n_items))}
    pending = {}
    for i in range(n_items):
        if i + ahead < n_items:
            s[i + ahead] = scores(i + ahead)
        pending[i] = softmax(i, s.pop(i))
        if i >= lag:
            accumulate(i - lag, *pending.pop(i - lag))
    for i in sorted(pending):
        accumulate(i, *pending[i])


def _flash_kernel(q_ref, k_ref, v_ref, o_ref, m_ref, acc_ref, *, group, rows_per_item):
    kv = pl.program_id(3)

    @pl.when(kv == 0)
    def _():
        m_ref[...] = jnp.full(m_ref.shape, NEG, F32)
        acc_ref[...] = jnp.zeros(acc_ref.shape, F32)

    k = k_ref[0]
    v = v_ref[0]
    rc = rows_per_item
    items = [(r, c) for r in range(group) for c in range(q_ref.shape[0] // rc)]

    def scores(i):
        r, c = items[i]
        q = q_ref[c * rc:(c + 1) * rc, r * HEAD_DIM:(r + 1) * HEAD_DIM]
        return lax.dot_general(q, k, NT_DIMS, preferred_element_type=F32)

    def softmax(i, s):
        r, c = items[i]
        rows = slice(c * rc, (c + 1) * rc)
        m_prev = m_ref[r, rows]
        m_new = jnp.maximum(m_prev, jnp.max(s, axis=1, keepdims=True))
        m_ref[r, rows] = m_new
        return jnp.exp2((s - m_new[:, :1]).astype(BF16)), jnp.exp2(m_prev - m_new)

    def accumulate(i, p, alpha):
        r, c = items[i]
        rows = slice(c * rc, (c + 1) * rc)
        acc_ref[r, rows] = (jnp.concatenate([alpha, alpha], axis=1) * acc_ref[r, rows]
                            + jnp.dot(p, v, preferred_element_type=F32))

    _pipelined(len(items), scores, softmax, accumulate)

    @pl.when(kv == pl.num_programs(3) - 1)
    def _():
        for r in range(group):
            a = acc_ref[r]
            o_ref[:, r * HEAD_DIM:(r + 1) * HEAD_DIM] = (a[:, :HEAD_DIM] / a[:, HEAD_DIM:HEAD_DIM + 1]).astype(o_ref.dtype)


def with_ones_column(v):
    b, sk, w = v.shape
    g = w // HEAD_DIM
    tail = jnp.zeros((b, sk, g, HEAD_DIM), v.dtype).at[..., 0].set(1)
    return jnp.concatenate([v.reshape(b, sk, g, HEAD_DIM), tail], axis=-1).reshape(b, sk, 2 * w)


def flash_gqa(q, k, v_ext, batch, n_kv_heads, group, k_col=0):
    sq = q.shape[0] // batch
    sk = k.shape[1]
    tq = _tile(sq, FLASH_TQ)
    tk = _tile(sk, FLASH_TK)
    rc = _tile(tq, FLASH_ITEM_ROWS)
    assert tk % BF16_SUBLANES == 0 and k_col % HEAD_DIM == 0
    nq = sq // tq
    kern = functools.partial(_flash_kernel, group=group, rows_per_item=rc)
    return pl.pallas_call(
        kern,
        out_shape=jax.ShapeDtypeStruct(q.shape, BF16),
        grid=(batch, n_kv_heads, nq, sk // tk),
        in_specs=[pl.BlockSpec((tq, group * HEAD_DIM), lambda b, g, i, j: (b * nq + i, g)),
                  pl.BlockSpec((1, tk, HEAD_DIM), lambda b, g, i, j: (b, j, k_col // HEAD_DIM + g)),
                  pl.BlockSpec((1, tk, 2 * HEAD_DIM), lambda b, g, i, j: (b, j, g))],
        out_specs=pl.BlockSpec((tq, group * HEAD_DIM), lambda b, g, i, j: (b * nq + i, g)),
        scratch_shapes=[pltpu.VMEM((group, tq, HEAD_DIM), F32),
                        pltpu.VMEM((group, tq, 2 * HEAD_DIM), F32)],
        compiler_params=_params("parallel", "parallel", "parallel", "arbitrary"),
        name="flash_gqa",
    )(q, k, v_ext)


def _diff_kernel(q_ref, k_ref, v_ref, lq1_ref, lk1_ref, lq2_ref, lk2_ref, g_ref, o_ref,
                 m_ref, l_ref, acc_ref, *, lam_init, rows_per_item):
    kv = pl.program_id(3)

    @pl.when(kv == 0)
    def _():
        m_ref[...] = jnp.full(m_ref.shape, NEG, F32)
        l_ref[...] = jnp.zeros(l_ref.shape, F32)
        acc_ref[...] = jnp.zeros(acc_ref.shape, F32)

    v = v_ref[0]
    rc = rows_per_item
    items = [(c, t) for c in range(2) for t in range(q_ref.shape[0] // rc)]

    def scores(i):
        c, t = items[i]
        sl = slice(c * HEAD_DIM, (c + 1) * HEAD_DIM)
        return lax.dot_general(q_ref[t * rc:(t + 1) * rc, sl], k_ref[0, :, sl], NT_DIMS, preferred_element_type=F32)

    def softmax(i, s):
        c, t = items[i]
        rows = slice(t * rc, (t + 1) * rc)
        m_prev = m_ref[c, rows]
        m_new = jnp.maximum(m_prev, jnp.max(s, axis=1, keepdims=True))
        alpha = jnp.exp2(m_prev - m_new)
        p = jnp.exp2(s - m_new[:, :1])
        part = p[:, :HEAD_DIM]
        for j in range(1, p.shape[1] // HEAD_DIM):
            part = part + p[:, j * HEAD_DIM:(j + 1) * HEAD_DIM]
        l_ref[c, rows] = alpha * l_ref[c, rows] + part
        m_ref[c, rows] = m_new
        return p.astype(BF16), alpha

    def accumulate(i, p, alpha):
        c, t = items[i]
        rows = slice(t * rc, (t + 1) * rc)
        acc_ref[c, rows] = (jnp.concatenate([alpha, alpha], axis=1) * acc_ref[c, rows]
                            + jnp.dot(p, v, preferred_element_type=F32))

    _pipelined(len(items), scores, softmax, accumulate)

    @pl.when(kv == pl.num_programs(3) - 1)
    def _():
        lam = (jnp.exp(jnp.sum(lq1_ref[...] * lk1_ref[...], axis=-1, keepdims=True))
               - jnp.exp(jnp.sum(lq2_ref[...] * lk2_ref[...], axis=-1, keepdims=True)) + lam_init)
        l1 = jnp.sum(l_ref[0], axis=-1, keepdims=True)
        l2 = jnp.sum(l_ref[1], axis=-1, keepdims=True)
        o = acc_ref[0] / l1 - lam * (acc_ref[1] / l2)
        o = o * lax.rsqrt(jnp.mean(o * o, axis=-1, keepdims=True) + EPS) * g_ref[...]
        o_ref[...] = (o * (1.0 - lam_init)).astype(o_ref.dtype)


def flash_diff(q, k, v, lam_vecs, subln_g, lam_init, batch, n_heads):
    sq = q.shape[0] // batch
    sk = k.shape[1]
    tq = _tile(sq, FLASH_TQ)
    tk = _tile(sk, FLASH_TK)
    assert tk % HEAD_DIM == 0
    nq = sq // tq
    w = 2 * HEAD_DIM
    vec_spec = pl.BlockSpec((1, HEAD_DIM), lambda b, h, i, j: (0, 0))
    kern = functools.partial(_diff_kernel, lam_init=lam_init, rows_per_item=_tile(tq, FLASH_ITEM_ROWS))
    return pl.pallas_call(
        kern,
        out_shape=jax.ShapeDtypeStruct(q.shape, BF16),
        grid=(batch, n_heads, nq, sk // tk),
        in_specs=[pl.BlockSpec((tq, w), lambda b, h, i, j: (b * nq + i, h)),
                  pl.BlockSpec((1, tk, w), lambda b, h, i, j: (b, j, h)),
                  pl.BlockSpec((1, tk, w), lambda b, h, i, j: (b, j, h)),
                  vec_spec, vec_spec, vec_spec, vec_spec,
                  pl.BlockSpec((1, w), lambda b, h, i, j: (0, 0))],
        out_specs=pl.BlockSpec((tq, w), lambda b, h, i, j: (b * nq + i, h)),
        scratch_shapes=[pltpu.VMEM((2, tq, HEAD_DIM), F32),
                        pltpu.VMEM((2, tq, HEAD_DIM), F32),
                        pltpu.VMEM((2, tq, w), F32)],
        compiler_params=_params("parallel", "parallel", "parallel", "arbitrary"),
        name="flash_diff",
    )(q, k, v, *[t.reshape(1, HEAD_DIM).astype(F32) for t in lam_vecs], subln_g.reshape(1, w).astype(F32))


def _shift_rows(x, first_row, last_row):
    n = x.shape[0]
    row = lax.broadcasted_iota(jnp.int32, x.shape, 0)
    up = jnp.where(row == 0, first_row, pltpu.roll(x, 1, 0))
    dn = jnp.where(row == n - 1, last_row, pltpu.roll(x, n - 1, 0))
    return up, dn


def _sconv_kernel(u_ref, gb_ref, gc_ref, up_ref, gcp_ref, un_ref, gcn_ref, w_ref, o_ref, *, tiles_per_seq):
    i = pl.program_id(0) % tiles_per_seq
    has_prev = (i != 0).astype(F32)
    has_next = (i != tiles_per_seq - 1).astype(F32)
    x = gc_ref[...].astype(F32) * u_ref[...].astype(F32)
    prev = (gcp_ref[...].astype(F32) * up_ref[...].astype(F32))[BF16_SUBLANES - 1:, :] * has_prev
    nxt = (gcn_ref[...].astype(F32) * un_ref[...].astype(F32))[:1, :] * has_next
    x_up, x_dn = _shift_rows(x, prev, nxt)
    w = w_ref[...]
    y = x_up * w[0:1] + x * w[1:2] + x_dn * w[2:3]
    o_ref[...] = (gb_ref[...].astype(F32) * y).astype(o_ref.dtype)


def short_conv(p, conv_w, seq):
    m = p.shape[0]
    c = conv_w.shape[1]
    tm = _tile(seq, 512)
    hb = BF16_SUBLANES
    n_halo = m // hb
    kern = functools.partial(_sconv_kernel, tiles_per_seq=seq // tm)

    def prev_spec(col):
        return pl.BlockSpec((hb, c), lambda i: (jnp.maximum(i * (tm // hb) - 1, 0), col))

    def next_spec(col):
        return pl.BlockSpec((hb, c), lambda i: (jnp.minimum((i + 1) * (tm // hb), n_halo - 1), col))

    return pl.pallas_call(
        kern,
        out_shape=jax.ShapeDtypeStruct((m, c), BF16),
        grid=(m // tm,),
        in_specs=[pl.BlockSpec((tm, c), lambda i: (i, 0)),
                  pl.BlockSpec((tm, c), lambda i: (i, 1)),
                  pl.BlockSpec((tm, c), lambda i: (i, 2)),
                  prev_spec(0), prev_spec(2), next_spec(0), next_spec(2),
                  pl.BlockSpec((3, c), lambda i: (0, 0))],
        out_specs=pl.BlockSpec((tm, c), lambda i: (i, 0)),
        compiler_params=_params("parallel"),
        name="short_conv",
    )(p, p, p, p, p, p, p, conv_w.astype(F32))


def _outproj_kernel(a_ref, b_ref, wa_ref, wb_ref, x_ref, g_ref, o_ref):
    y = jnp.dot(a_ref[...], wa_ref[...], preferred_element_type=F32)
    y += jnp.dot(b_ref[...], wb_ref[...], preferred_element_type=F32)
    o_ref[...] = x_ref[...] + g_ref[0] * y


def out_proj_residual(x, a, b, w, gate, rows_per_mod):
    m, d = x.shape
    ka, kb = a.shape[1], b.shape[1]
    tm = _tile(math.gcd(m, rows_per_mod), 512)
    return pl.pallas_call(
        _outproj_kernel,
        out_shape=jax.ShapeDtypeStruct((m, d), F32),
        grid=(m // tm,),
        in_specs=[pl.BlockSpec((tm, ka), lambda i: (i, 0)),
                  pl.BlockSpec((tm, kb), lambda i: (i, 0)),
                  pl.BlockSpec((ka, d), lambda i: (0, 0)),
                  pl.BlockSpec((kb, d), lambda i: (0, 0)),
                  pl.BlockSpec((tm, d), lambda i: (i, 0)),
                  pl.BlockSpec((1, 1, d), lambda i: (i * tm // rows_per_mod, 0, 0))],
        out_specs=pl.BlockSpec((tm, d), lambda i: (i, 0)),
        compiler_params=_params("parallel"),
        name="out_proj_residual",
    )(a, b, w[:ka], w[ka:], x, gate)


def _ffn_kernel(x_ref, xp_ref, xn_ref, g_ref, sh_ref, sc_ref, gate_ref, wg_ref, wu_ref, cw_ref, cb_ref, wd_ref,
                *rest, tiles_per_seq, final_norm):
    if final_norm:
        fg_ref, o_ref, h_ref, hp_ref, hn_ref, acc_ref = rest
    else:
        o_ref, h_ref, hp_ref, hn_ref, acc_ref = rest
    j = pl.program_id(1)
    i = pl.program_id(0) % tiles_per_seq

    @pl.when(j == 0)
    def _():
        g, sh, sc = g_ref[...], sh_ref[0], sc_ref[0]
        h_ref[...] = _norm_mod(x_ref[...], g, sh, sc).astype(BF16)
        hp_ref[...] = _norm_mod(xp_ref[...], g, sh, sc).astype(BF16)
        hn_ref[...] = _norm_mod(xn_ref[...], g, sh, sc).astype(BF16)
        acc_ref[...] = jnp.zeros(acc_ref.shape, F32)

    wg = wg_ref[...]
    gm = jnp.dot(h_ref[...], wg, preferred_element_type=F32)
    has_prev = (i != 0).astype(F32)
    has_next = (i != tiles_per_seq - 1).astype(F32)
    g_prev = jnp.dot(hp_ref[...], wg, preferred_element_type=F32)[F32_SUBLANES - 1:, :] * has_prev
    g_next = jnp.dot(hn_ref[...], wg, preferred_element_type=F32)[:1, :] * has_next
    g_up, g_dn = _shift_rows(gm, g_prev, g_next)
    cw = cw_ref[...]
    conv = g_up * cw[0:1] + gm * cw[1:2] + g_dn * cw[2:3] + cb_ref[...]
    up = jnp.dot(h_ref[...], wu_ref[...], preferred_element_type=F32)
    act = (conv * jax.nn.sigmoid(conv) * up).astype(BF16)
    acc_ref[...] += jnp.dot(act, wd_ref[...], preferred_element_type=F32)

    @pl.when(j == pl.num_programs(1) - 1)
    def _():
        y = x_ref[...] + gate_ref[0] * acc_ref[...]
        if final_norm:
            y = y * lax.rsqrt(jnp.mean(y * y, axis=-1, keepdims=True) + EPS) * fg_ref[...]
        o_ref[...] = y


def conv_glu_residual(x, gain, shift, scale, gate, w_gate, w_up, conv_w, conv_b, w_down, seq, rows_per_mod,
                      final_gain=None):
    m, d = x.shape
    f = w_gate.shape[1]
    tm = _tile(math.gcd(seq, rows_per_mod), 512)
    tf = _tile(f, 512)
    hb = F32_SUBLANES
    n_halo = m // hb
    mod_spec = pl.BlockSpec((1, 1, d), lambda i, j: (i * tm // rows_per_mod, 0, 0))
    row_spec = pl.BlockSpec((1, d), lambda i, j: (0, 0))
    in_specs = [pl.BlockSpec((tm, d), lambda i, j: (i, 0)),
                pl.BlockSpec((hb, d), lambda i, j: (jnp.maximum(i * (tm // hb) - 1, 0), 0)),
                pl.BlockSpec((hb, d), lambda i, j: (jnp.minimum((i + 1) * (tm // hb), n_halo - 1), 0)),
                row_spec, mod_spec, mod_spec, mod_spec,
                pl.BlockSpec((d, tf), lambda i, j: (0, j)),
                pl.BlockSpec((d, tf), lambda i, j: (0, j)),
                pl.BlockSpec((3, tf), lambda i, j: (0, j)),
                pl.BlockSpec((1, tf), lambda i, j: (0, j)),
                pl.BlockSpec((tf, d), lambda i, j: (j, 0))]
    args = [x, x, x, gain.reshape(1, d), shift, scale, gate, w_gate, w_up,
            conv_w.astype(F32), conv_b.reshape(1, f).astype(F32), w_down]
    if final_gain is not None:
        in_specs.append(row_spec)
        args.append(final_gain.reshape(1, d).astype(F32))
    kern = functools.partial(_ffn_kernel, tiles_per_seq=seq // tm, final_norm=final_gain is not None)
    return pl.pallas_call(
        kern,
        out_shape=jax.ShapeDtypeStruct((m, d), F32),
        grid=(m // tm, f // tf),
        in_specs=in_specs,
        out_specs=pl.BlockSpec((tm, d), lambda i, j: (i, 0)),
        scratch_shapes=[pltpu.VMEM((tm, d), BF16), pltpu.VMEM((hb, d), BF16), pltpu.VMEM((hb, d), BF16),
                        pltpu.VMEM((tm, d), F32)],
        compiler_params=_params("parallel", "arbitrary"),
        name="conv_glu_residual",
    )(*args)


def kernel(x, c, ctx, c_ctx, ada_w, ada_b, norm_g, ab_w_in, ab_w_out, na_rpb, gqa_q_gain, gqa_k_gain, cd_w_in,
           cd_w_out, sconv_w, diff_lq1, diff_lk1, diff_lq2, diff_lk2, diff_subln_g, ffn_w_gate, ffn_w_up,
           ffn_conv_w, ffn_conv_b, ffn_w_down, final_g):
    batch, seq, d = x.shape
    ctx_len = ctx.shape[1]
    depth = ada_w.shape[0]
    assert depth == 2 and batch < F32_SUBLANES
    a_w = na_rpb.shape[1] * HEAD_DIM
    b_w = ab_w_out.shape[1] - a_w
    b_kv_w = (ab_w_in.shape[2] - 3 * a_w - b_w) // 2
    n_kv = b_kv_w // HEAD_DIM
    group = b_w // b_kv_w
    ab_kv = a_w + b_w
    c_w = sconv_w.shape[2]
    d_vw = cd_w_out.shape[1] - c_w
    d_heads = d_vw // (2 * HEAD_DIM)
    cd_kv = 3 * c_w + d_vw
    assert c_w == d_vw and cd_w_in.shape[2] == cd_kv + 2 * d_vw
    scale = HEAD_DIM ** -0.5
    ml, mc = batch * seq, batch * ctx_len

    cvec = jnp.zeros((F32_SUBLANES, d), F32).at[:batch].set(c).at[batch].set(c_ctx)
    mods = ada_vectors(cvec, ada_w, ada_b).reshape(depth, F32_SUBLANES, 6, 1, d)
    rope = rope_tables(seq)
    x = x.reshape(ml, d)
    h_ctx = ctx.reshape(mc, d)

    for layer in range(depth):
        last = layer == depth - 1
        i = layer // 2
        lat = [mods[layer, :batch, t] for t in range(6)]
        cm = [mods[layer, batch:batch + 1, t] for t in range(6)]
        if layer % 2 == 0:
            w_in = ab_w_in[i].astype(BF16)
            p = mod_matmul(x, norm_g[layer, 0], lat[0], lat[1], w_in, seq)
            pc = mod_matmul(h_ctx, norm_g[layer, 0], cm[0], cm[1], w_in, mc)
            bq = head_prep(p, a_w, b_w, gain=gqa_q_gain[i], rope=rope, scale=scale * LOG2E, seq=seq)
            bk = head_prep(p, ab_kv + 2 * a_w, b_kv_w, gain=gqa_k_gain[i], rope=rope, seq=seq)
            bck = head_prep(pc, ab_kv + 2 * a_w, b_kv_w, gain=gqa_k_gain[i])
            v_col = ab_kv + 2 * a_w + b_kv_w
            pc3 = pc.reshape(batch, ctx_len, pc.shape[1])
            k_all = jnp.concatenate([bck.reshape(batch, ctx_len, b_kv_w), bk.reshape(batch, seq, b_kv_w)], axis=1)
            vc_ext = with_ones_column(pc3[:, :, v_col:])
            v_all = jnp.concatenate([vc_ext, with_ones_column(p[:, v_col:].reshape(batch, seq, b_kv_w))], axis=1)
            o_a = neighbourhood_attention(p, pc, na_rpb[i], batch, seq, ctx_len, a_w,
                                          ab_kv, ab_kv + a_w, ab_kv, ab_kv + a_w)
            o_b = flash_gqa(bq, k_all, v_all, batch, n_kv, group)
            w_out = ab_w_out[i].astype(BF16)
            x = out_proj_residual(x, o_a, o_b, w_out, lat[2], seq)
            if not last:
                acq = head_prep(pc, 0, a_w, scale=scale * LOG2E)
                bcq = head_prep(pc, a_w, b_w, gain=gqa_q_gain[i], scale=scale * LOG2E)
                oc_a = flash_gqa(acq, pc3, with_ones_column(pc3[:, :, ab_kv + a_w:ab_kv + 2 * a_w]), batch,
                                 a_w // HEAD_DIM, 1, k_col=ab_kv)
                oc_b = flash_gqa(bcq, bck.reshape(batch, ctx_len, b_kv_w), vc_ext, batch, n_kv, group)
                h_ctx = out_proj_residual(h_ctx, oc_a, oc_b, w_out, cm[2], mc)
        else:
            assert last, "a non-final short-conv / differential layer is not needed at this depth"
            lam_init = 0.8 - 0.6 * math.exp(-0.3 * layer)
            w_in = cd_w_in[i].astype(BF16)
            p = mod_matmul(x, norm_g[layer, 0], lat[0], lat[1], w_in, seq)
            pc = mod_matmul(h_ctx, norm_g[layer, 0], cm[0], cm[1], w_in[:, cd_kv:], mc)
            dq = head_prep(p, 3 * c_w, d_vw, rope=rope, scale=scale * LOG2E, seq=seq)
            dk = head_prep(p, cd_kv, d_vw, rope=rope, seq=seq)
            k_all = jnp.concatenate([pc[:, :d_vw].reshape(batch, ctx_len, d_vw), dk.reshape(batch, seq, d_vw)], axis=1)
            v_all = jnp.concatenate([pc[:, d_vw:].reshape(batch, ctx_len, d_vw),
                                     p[:, cd_kv + d_vw:].reshape(batch, seq, d_vw)], axis=1)
            y_c = short_conv(p, sconv_w[i], seq)
            o_d = flash_diff(dq, k_all, v_all, (diff_lq1[i], diff_lk1[i], diff_lq2[i], diff_lk2[i]),
                             diff_subln_g[i], lam_init, batch, d_heads)
            x = out_proj_residual(x, y_c, o_d, cd_w_out[i].astype(BF16), lat[2], seq)
        ffn_w = (ffn_w_gate[layer].astype(BF16), ffn_w_up[layer].astype(BF16), ffn_conv_w[layer],
                 ffn_conv_b[layer], ffn_w_down[layer].astype(BF16))
        x = conv_glu_residual(x, norm_g[layer, 1], lat[3], lat[4], lat[5], *ffn_w, seq, seq,
                              final_gain=final_g if last else None)
        if not last:
            h_ctx = conv_glu_residual(h_ctx, norm_g[layer, 1], cm[3], cm[4], cm[5], *ffn_w, ctx_len, mc)
    return x.reshape(batch, seq, d)
```

```python
import functools
import math

import numpy as np
import jax
import jax.numpy as jnp
from jax import lax
from jax.experimental import pallas as pl
from jax.experimental.pallas import tpu as pltpu

HEAD_DIM = 128
GRID_W = 64
ROPE_THETA = 10000.0
EPS = 1e-6
F32 = jnp.float32
BF16 = jnp.bfloat16
NEG = -1e30
VMEM_LIMIT_BYTES = 60 * 1024 * 1024
BF16_SUBLANES = 16
F32_SUBLANES = 8
NT_DIMS = (((1,), (1,)), ((), ()))
FLASH_TQ = 1024
FLASH_TK = 3328
FLASH_ITEM_ROWS = 512
FFN_TM = 1024
FFN_TF = 512
LOG2E = math.log2(math.e)


def _params(*sem):
    return pltpu.CompilerParams(dimension_semantics=sem, vmem_limit_bytes=VMEM_LIMIT_BYTES)


def _tile(n, want):
    t = min(n, want)
    while n % t:
        t -= 1
    return t


def _norm_mod(x, g, sh, sc):
    y = x * lax.rsqrt(jnp.mean(x * x, axis=-1, keepdims=True) + EPS)
    return (y * g) * (1.0 + sc) + sh


def _ada_kernel(c_ref, w_ref, b_ref, o_ref):
    c = c_ref[...]
    a = c * jax.nn.sigmoid(c)
    a_hi = a.astype(BF16)
    a_lo = (a - a_hi.astype(F32)).astype(BF16)
    w = w_ref[0]
    w_hi = w.astype(BF16)
    w_lo = (w - w_hi.astype(F32)).astype(BF16)
    acc = jnp.dot(a_hi, w_hi, preferred_element_type=F32)
    acc += jnp.dot(a_lo, w_hi, preferred_element_type=F32)
    acc += jnp.dot(a_hi, w_lo, preferred_element_type=F32)
    o_ref[0] = acc + b_ref[0]


def ada_vectors(cvec, ada_w, ada_b):
    depth, d, n = ada_w.shape
    tn = _tile(n, 512)
    return pl.pallas_call(
        _ada_kernel,
        out_shape=jax.ShapeDtypeStruct((depth, F32_SUBLANES, n), F32),
        grid=(depth, n // tn),
        in_specs=[pl.BlockSpec((F32_SUBLANES, d), lambda l, j: (0, 0)),
                  pl.BlockSpec((1, d, tn), lambda l, j: (l, 0, j)),
                  pl.BlockSpec((1, 1, tn), lambda l, j: (l, 0, j))],
        out_specs=pl.BlockSpec((1, F32_SUBLANES, tn), lambda l, j: (l, 0, j)),
        compiler_params=_params("parallel", "parallel"),
        name="ada_vectors",
    )(cvec, ada_w, ada_b.reshape(depth, 1, n))


def _modmm_kernel(x_ref, g_ref, sh_ref, sc_ref, w_ref, o_ref, h_ref):
    @pl.when(pl.program_id(1) == 0)
    def _():
        h_ref[...] = _norm_mod(x_ref[...], g_ref[...], sh_ref[0], sc_ref[0]).astype(BF16)

    o_ref[...] = jnp.dot(h_ref[...], w_ref[...], preferred_element_type=F32).astype(o_ref.dtype)


def mod_matmul(x, gain, shift, scale, w, rows_per_mod):
    m, d = x.shape
    n = w.shape[1]
    tm = _tile(math.gcd(m, rows_per_mod), 1024)
    tn = _tile(n, 512)
    mod_spec = pl.BlockSpec((1, 1, d), lambda i, j: (i * tm // rows_per_mod, 0, 0))
    return pl.pallas_call(
        _modmm_kernel,
        out_shape=jax.ShapeDtypeStruct((m, n), BF16),
        grid=(m // tm, n // tn),
        in_specs=[pl.BlockSpec((tm, d), lambda i, j: (i, 0)),
                  pl.BlockSpec((1, d), lambda i, j: (0, 0)),
                  mod_spec, mod_spec,
                  pl.BlockSpec((d, tn), lambda i, j: (0, j))],
        out_specs=pl.BlockSpec((tm, tn), lambda i, j: (i, j)),
        scratch_shapes=[pltpu.VMEM((tm, d), BF16)],
        compiler_params=_params("parallel", "arbitrary"),
        name="mod_matmul",
    )(x, gain.reshape(1, d), shift, scale, w)


def _prep_kernel(*refs, n_heads, use_gain, use_rope, scale):
    refs = list(refs)
    x_ref = refs.pop(0)
    gain = refs.pop(0)[...] if use_gain else None
    if use_rope:
        cos, sin_a, sin_b = refs.pop(0)[...], refs.pop(0)[...], refs.pop(0)[...]
    o_ref = refs.pop(0)
    for h in range(n_heads):
        sl = slice(h * HEAD_DIM, (h + 1) * HEAD_DIM)
        x = x_ref[:, sl].astype(F32)
        if use_gain:
            x = x * lax.rsqrt(jnp.mean(x * x, axis=-1, keepdims=True) + EPS) * gain
        if use_rope:
            x = (x * cos + pltpu.roll(x, HEAD_DIM - HEAD_DIM // 4, 1) * sin_a
                 + pltpu.roll(x, HEAD_DIM // 4, 1) * sin_b)
        if scale != 1.0:
            x = x * scale
        o_ref[:, sl] = x.astype(o_ref.dtype)


def head_prep(p, col_start, width, gain=None, rope=None, scale=1.0, seq=None):
    m = p.shape[0]
    assert col_start % width == 0
    tm = _tile(seq if rope is not None else m, 512)
    args = [p]
    in_specs = [pl.BlockSpec((tm, width), lambda i: (i, col_start // width))]
    if gain is not None:
        args.append(gain.reshape(1, HEAD_DIM).astype(F32))
        in_specs.append(pl.BlockSpec((1, HEAD_DIM), lambda i: (0, 0)))
    if rope is not None:
        n_seq_tiles = seq // tm
        args.extend(rope)
        in_specs.extend([pl.BlockSpec((tm, HEAD_DIM), lambda i: (i % n_seq_tiles, 0))] * 3)
    kern = functools.partial(_prep_kernel, n_heads=width // HEAD_DIM, use_gain=gain is not None,
                             use_rope=rope is not None, scale=scale)
    return pl.pallas_call(
        kern,
        out_shape=jax.ShapeDtypeStruct((m, width), BF16),
        grid=(m // tm,),
        in_specs=in_specs,
        out_specs=pl.BlockSpec((tm, width), lambda i: (i, 0)),
        compiler_params=_params("parallel"),
        name="head_prep",
    )(*args)


def rope_tables(seq):
    t = jnp.arange(seq)
    axis_dim = HEAD_DIM // 2
    inv = 1.0 / (ROPE_THETA ** (jnp.arange(0, axis_dim, 2, dtype=F32) / axis_dim))
    ang_r = (t // GRID_W).astype(F32)[:, None] * inv
    ang_c = (t % GRID_W).astype(F32)[:, None] * inv
    ang = jnp.concatenate([ang_r, ang_r, ang_c, ang_c], axis=-1)
    cos, sin = jnp.cos(ang), jnp.sin(ang)
    first_half = (np.arange(HEAD_DIM) % (HEAD_DIM // 2)) < HEAD_DIM // 4
    sin_a = jnp.where(first_half[None], -sin, 0.0)
    sin_b = jnp.where(first_half[None], 0.0, sin)
    return cos, sin_a, sin_b


def _na_kernel(q_ref, k0_ref, k1_ref, k2_ref, v0_ref, v1_ref, v2_ref, kc_ref, vc_ref, bias_ref, o_ref,
               *, n_heads, scale, tq):
    k_refs = (k0_ref, k1_ref, k2_ref)
    v_refs = (v0_ref, v1_ref, v2_ref)
    for h in range(n_heads):
        sl = slice(h * HEAD_DIM, (h + 1) * HEAD_DIM)
        q = q_ref[:, sl]
        s = [lax.dot_general(q, k_refs[j][:, sl], NT_DIMS, preferred_element_type=F32) * scale
             + bias_ref[0, h, :, j * tq:(j + 1) * tq] for j in range(3)]
        s.append(lax.dot_general(q, kc_ref[:, sl], NT_DIMS, preferred_element_type=F32) * scale)
        m = functools.reduce(jnp.maximum, [jnp.max(t, axis=1, keepdims=True) for t in s])
        p = [jnp.exp(t - m) for t in s]
        l = functools.reduce(jnp.add, [jnp.sum(t, axis=1, keepdims=True) for t in p])
        vs = [v_refs[j][:, sl] for j in range(3)] + [vc_ref[:, sl]]
        o = functools.reduce(jnp.add, [jnp.dot(pj.astype(BF16), vj, preferred_element_type=F32)
                                       for pj, vj in zip(p, vs)])
        o_ref[:, sl] = (o / l).astype(o_ref.dtype)


def _na_bias_table(rpb, rows, na_rows, na_cols):
    rq_rows = na_rows // 2
    n_blk = rows // rq_rows
    n_heads = rpb.shape[0]
    w = GRID_W
    assert na_cols <= w
    v = jnp.concatenate([rpb[:, :, na_cols - 1:], jnp.zeros((n_heads, rpb.shape[1], 2 * w - rpb.shape[2]), rpb.dtype),
                         rpb[:, :, :na_cols - 1]], axis=-1).astype(F32)
    toep = jnp.tile(v, (1, 1, w + 1))[:, :, :w * (2 * w - 1)].reshape(n_heads, rpb.shape[1], w, 2 * w - 1)[..., :w]
    tabs = []
    for qb, kb in ((0, 0), (1, 0), (n_blk - 1, n_blk - 3)):
        rq = np.repeat(qb * rq_rows + np.arange(rq_rows), w)[:, None]
        cq = np.tile(np.arange(w), rq_rows)[:, None]
        rk = np.repeat(kb * rq_rows + np.arange(3 * rq_rows), w)[None, :]
        ck = np.tile(np.arange(w), 3 * rq_rows)[None, :]
        r0 = np.clip(rq - na_rows // 2, 0, rows - na_rows)
        c0 = np.clip(cq - na_cols // 2, 0, w - na_cols)
        valid = (rk >= r0) & (rk < r0 + na_rows) & (ck >= c0) & (ck < c0 + na_cols)
        row_blocks = []
        for i in range(rq_rows):
            dr = [int(np.clip((kb - qb) * rq_rows + j - i + na_rows - 1, 0, 2 * na_rows - 2)) for j in range(3 * rq_rows)]
            row_blocks.append(jnp.concatenate([toep[:, d] for d in dr], axis=-1))
        tabs.append(jnp.where(valid[None], jnp.concatenate(row_blocks, axis=1), NEG))
    return jnp.stack(tabs)


def neighbourhood_attention(p_lat, p_ctx, rpb, batch, seq, ctx_len, a_w, k_col, v_col, ck_col, cv_col):
    n_heads = a_w // HEAD_DIM
    na_rows, na_cols = (rpb.shape[1] + 1) // 2, (rpb.shape[2] + 1) // 2
    rows = seq // GRID_W
    rq_rows = na_rows // 2
    tq = rq_rows * GRID_W
    n_blk = rows // rq_rows
    assert rows >= na_rows and rows % rq_rows == 0 and n_blk >= 3 and na_rows % 2 == 0
    assert k_col % a_w == 0 and v_col % a_w == 0 and ck_col % a_w == 0 and cv_col % a_w == 0
    bias = _na_bias_table(rpb, rows, na_rows, na_cols)

    def kv_spec(col, j):
        return pl.BlockSpec((tq, a_w), lambda b, r: (b * n_blk + jnp.clip(r - 1, 0, n_blk - 3) + j, col // a_w))

    kern = functools.partial(_na_kernel, n_heads=n_heads, scale=HEAD_DIM ** -0.5, tq=tq)
    return pl.pallas_call(
        kern,
        out_shape=jax.ShapeDtypeStruct((batch * seq, a_w), BF16),
        grid=(batch, n_blk),
        in_specs=[pl.BlockSpec((tq, a_w), lambda b, r: (b * n_blk + r, 0)),
                  kv_spec(k_col, 0), kv_spec(k_col, 1), kv_spec(k_col, 2),
                  kv_spec(v_col, 0), kv_spec(v_col, 1), kv_spec(v_col, 2),
                  pl.BlockSpec((ctx_len, a_w), lambda b, r: (b, ck_col // a_w)),
                  pl.BlockSpec((ctx_len, a_w), lambda b, r: (b, cv_col // a_w)),
                  pl.BlockSpec((1, n_heads, tq, 3 * tq),
                               lambda b, r: (jnp.where(r == 0, 0, jnp.where(r == n_blk - 1, 2, 1)), 0, 0, 0))],
        out_specs=pl.BlockSpec((tq, a_w), lambda b, r: (b * n_blk + r, 0)),
        compiler_params=_params("parallel", "arbitrary"),
        name="neighbourhood_attention",
    )(p_lat, p_lat, p_lat, p_lat, p_lat, p_lat, p_lat, p_ctx, p_ctx, bias)


def _pipelined(n_items, scores, softmax, accumulate):
    ahead, lag = 2, 1
    s = {j: scores(j) for j in range(min(ahead, n_items))}
    pending = {}
    for i in range(n_items):
        if i + ahead < n_items:
            s[i + ahead] = scores(i + ahead)
        pending[i] = softmax(i, s.pop(i))
        if i >= lag:
            accumulate(i - lag, *pending.pop(i - lag))
    for i in sorted(pending):
        accumulate(i, *pending[i])


def _flash_kernel(q_ref, k_ref, v_ref, o_ref, m_ref, acc_ref, *, group, rows_per_item):
    kv = pl.program_id(3)

    @pl.when(kv == 0)
    def _():
        m_ref[...] = jnp.full(m_ref.shape, NEG, F32)
        acc_ref[...] = jnp.zeros(acc_ref.shape, F32)

    k = k_ref[0]
    v = v_ref[0]
    rc = rows_per_item
    items = [(r, c) for r in range(group) for c in range(q_ref.shape[0] // rc)]

    def scores(i):
        r, c = items[i]
        q = q_ref[c * rc:(c + 1) * rc, r * HEAD_DIM:(r + 1) * HEAD_DIM]
        return lax.dot_general(q, k, NT_DIMS, preferred_element_type=F32)

    def softmax(i, s):
        r, c = items[i]
        rows = slice(c * rc, (c + 1) * rc)
        m_prev = m_ref[r, rows]
        m_new = jnp.maximum(m_prev, jnp.max(s, axis=1, keepdims=True))
        m_ref[r, rows] = m_new
        return jnp.exp2((s - m_new[:, :1]).astype(BF16)), jnp.exp2(m_prev - m_new)

    def accumulate(i, p, alpha):
        r, c = items[i]
        rows = slice(c * rc, (c + 1) * rc)
        acc_ref[r, rows] = (jnp.concatenate([alpha, alpha], axis=1) * acc_ref[r, rows]
                            + jnp.dot(p, v, preferred_element_type=F32))

    _pipelined(len(items), scores, softmax, accumulate)

    @pl.when(kv == pl.num_programs(3) - 1)
    def _():
        for r in range(group):
            a = acc_ref[r]
            o_ref[:, r * HEAD_DIM:(r + 1) * HEAD_DIM] = (a[:, :HEAD_DIM] / a[:, HEAD_DIM:HEAD_DIM + 1]).astype(o_ref.dtype)


def with_ones_column(v):
    b, sk, w = v.shape
    g = w // HEAD_DIM
    tail = jnp.zeros((b, sk, g, HEAD_DIM), v.dtype).at[..., 0].set(1)
    return jnp.concatenate([v.reshape(b, sk, g, HEAD_DIM), tail], axis=-1).reshape(b, sk, 2 * w)


def flash_gqa(q, k, v_ext, batch, n_kv_heads, group, k_col=0):
    sq = q.shape[0] // batch
    sk = k.shape[1]
    tq = _tile(sq, FLASH_TQ)
    tk = _tile(sk, FLASH_TK)
    rc = _tile(tq, FLASH_ITEM_ROWS)
    assert tk % BF16_SUBLANES == 0 and k_col % HEAD_DIM == 0
    nq = sq // tq
    kern = functools.partial(_flash_kernel, group=group, rows_per_item=rc)
    return pl.pallas_call(
        kern,
        out_shape=jax.ShapeDtypeStruct(q.shape, BF16),
        grid=(batch, n_kv_heads, nq, sk // tk),
        in_specs=[pl.BlockSpec((tq, group * HEAD_DIM), lambda b, g, i, j: (b * nq + i, g)),
                  pl.BlockSpec((1, tk, HEAD_DIM), lambda b, g, i, j: (b, j, k_col // HEAD_DIM + g)),
                  pl.BlockSpec((1, tk, 2 * HEAD_DIM), lambda b, g, i, j: (b, j, g))],
        out_specs=pl.BlockSpec((tq, group * HEAD_DIM), lambda b, g, i, j: (b * nq + i, g)),
        scratch_shapes=[pltpu.VMEM((group, tq, HEAD_DIM), F32),
                        pltpu.VMEM((group, tq, 2 * HEAD_DIM), F32)],
        compiler_params=_params("parallel", "parallel", "parallel", "arbitrary"),
        name="flash_gqa",
    )(q, k, v_ext)


def _diff_kernel(q_ref, k_ref, v_ref, lq1_ref, lk1_ref, lq2_ref, lk2_ref, g_ref, o_ref,
                 m_ref, l_ref, acc_ref, *, lam_init, rows_per_item):
    kv = pl.program_id(3)

    @pl.when(kv == 0)
    def _():
        m_ref[...] = jnp.full(m_ref.shape, NEG, F32)
        l_ref[...] = jnp.zeros(l_ref.shape, F32)
        acc_ref[...] = jnp.zeros(acc_ref.shape, F32)

    v = v_ref[0]
    rc = rows_per_item
    items = [(c, t) for c in range(2) for t in range(q_ref.shape[0] // rc)]

    def scores(i):
        c, t = items[i]
        sl = slice(c * HEAD_DIM, (c + 1) * HEAD_DIM)
        return lax.dot_general(q_ref[t * rc:(t + 1) * rc, sl], k_ref[0, :, sl], NT_DIMS, preferred_element_type=F32)

    def softmax(i, s):
        c, t = items[i]
        rows = slice(t * rc, (t + 1) * rc)
        m_prev = m_ref[c, rows]
        m_new = jnp.maximum(m_prev, jnp.max(s, axis=1, keepdims=True))
        alpha = jnp.exp2(m_prev - m_new)
        p = jnp.exp2(s - m_new[:, :1])
        part = p[:, :HEAD_DIM]
        for j in range(1, p.shape[1] // HEAD_DIM):
            part = part + p[:, j * HEAD_DIM:(j + 1) * HEAD_DIM]
        l_ref[c, rows] = alpha * l_ref[c, rows] + part
        m_ref[c, rows] = m_new
        return p.astype(BF16), alpha

    def accumulate(i, p, alpha):
        c, t = items[i]
        rows = slice(t * rc, (t + 1) * rc)
        acc_ref[c, rows] = (jnp.concatenate([alpha, alpha], axis=1) * acc_ref[c, rows]
                            + jnp.dot(p, v, preferred_element_type=F32))

    _pipelined(len(items), scores, softmax, accumulate)

    @pl.when(kv == pl.num_programs(3) - 1)
    def _():
        lam = (jnp.exp(jnp.sum(lq1_ref[...] * lk1_ref[...], axis=-1, keepdims=True))
               - jnp.exp(jnp.sum(lq2_ref[...] * lk2_ref[...], axis=-1, keepdims=True)) + lam_init)
        l1 = jnp.sum(l_ref[0], axis=-1, keepdims=True)
        l2 = jnp.sum(l_ref[1], axis=-1, keepdims=True)
        o = acc_ref[0] / l1 - lam * (acc_ref[1] / l2)
        o = o * lax.rsqrt(jnp.mean(o * o, axis=-1, keepdims=True) + EPS) * g_ref[...]
        o_ref[...] = (o * (1.0 - lam_init)).astype(o_ref.dtype)


def flash_diff(q, k, v, lam_vecs, subln_g, lam_init, batch, n_heads):
    sq = q.shape[0] // batch
    sk = k.shape[1]
    tq = _tile(sq, FLASH_TQ)
    tk = _tile(sk, FLASH_TK)
    assert tk % HEAD_DIM == 0
    nq = sq // tq
    w = 2 * HEAD_DIM
    vec_spec = pl.BlockSpec((1, HEAD_DIM), lambda b, h, i, j: (0, 0))
    kern = functools.partial(_diff_kernel, lam_init=lam_init, rows_per_item=_tile(tq, FLASH_ITEM_ROWS))
    return pl.pallas_call(
        kern,
        out_shape=jax.ShapeDtypeStruct(q.shape, BF16),
        grid=(batch, n_heads, nq, sk // tk),
        in_specs=[pl.BlockSpec((tq, w), lambda b, h, i, j: (b * nq + i, h)),
                  pl.BlockSpec((1, tk, w), lambda b, h, i, j: (b, j, h)),
                  pl.BlockSpec((1, tk, w), lambda b, h, i, j: (b, j, h)),
                  vec_spec, vec_spec, vec_spec, vec_spec,
                  pl.BlockSpec((1, w), lambda b, h, i, j: (0, 0))],
        out_specs=pl.BlockSpec((tq, w), lambda b, h, i, j: (b * nq + i, h)),
        scratch_shapes=[pltpu.VMEM((2, tq, HEAD_DIM), F32),
                        pltpu.VMEM((2, tq, HEAD_DIM), F32),
                        pltpu.VMEM((2, tq, w), F32)],
        compiler_params=_params("parallel", "parallel", "parallel", "arbitrary"),
        name="flash_diff",
    )(q, k, v, *[t.reshape(1, HEAD_DIM).astype(F32) for t in lam_vecs], subln_g.reshape(1, w).astype(F32))


def _shift_rows(x, first_row, last_row):
    n = x.shape[0]
    row = lax.broadcasted_iota(jnp.int32, x.shape, 0)
    up = jnp.where(row == 0, first_row, pltpu.roll(x, 1, 0))
    dn = jnp.where(row == n - 1, last_row, pltpu.roll(x, n - 1, 0))
    return up, dn


def _sconv_kernel(u_ref, gb_ref, gc_ref, up_ref, gcp_ref, un_ref, gcn_ref, w_ref, o_ref, *, tiles_per_seq):
    i = pl.program_id(0) % tiles_per_seq
    has_prev = (i != 0).astype(F32)
    has_next = (i != tiles_per_seq - 1).astype(F32)
    x = gc_ref[...].astype(F32) * u_ref[...].astype(F32)
    prev = (gcp_ref[...].astype(F32) * up_ref[...].astype(F32))[BF16_SUBLANES - 1:, :] * has_prev
    nxt = (gcn_ref[...].astype(F32) * un_ref[...].astype(F32))[:1, :] * has_next
    x_up, x_dn = _shift_rows(x, prev, nxt)
    w = w_ref[...]
    y = x_up * w[0:1] + x * w[1:2] + x_dn * w[2:3]
    o_ref[...] = (gb_ref[...].astype(F32) * y).astype(o_ref.dtype)


def short_conv(p, conv_w, seq):
    m = p.shape[0]
    c = conv_w.shape[1]
    tm = _tile(seq, 512)
    hb = BF16_SUBLANES
    n_halo = m // hb
    kern = functools.partial(_sconv_kernel, tiles_per_seq=seq // tm)

    def prev_spec(col):
        return pl.BlockSpec((hb, c), lambda i: (jnp.maximum(i * (tm // hb) - 1, 0), col))

    def next_spec(col):
        return pl.BlockSpec((hb, c), lambda i: (jnp.minimum((i + 1) * (tm // hb), n_halo - 1), col))

    return pl.pallas_call(
        kern,
        out_shape=jax.ShapeDtypeStruct((m, c), BF16),
        grid=(m // tm,),
        in_specs=[pl.BlockSpec((tm, c), lambda i: (i, 0)),
                  pl.BlockSpec((tm, c), lambda i: (i, 1)),
                  pl.BlockSpec((tm, c), lambda i: (i, 2)),
                  prev_spec(0), prev_spec(2), next_spec(0), next_spec(2),
                  pl.BlockSpec((3, c), lambda i: (0, 0))],
        out_specs=pl.BlockSpec((tm, c), lambda i: (i, 0)),
        compiler_params=_params("parallel"),
        name="short_conv",
    )(p, p, p, p, p, p, p, conv_w.astype(F32))


def _outproj_kernel(a_ref, b_ref, wa_ref, wb_ref, x_ref, g_ref, o_ref):
    y = jnp.dot(a_ref[...], wa_ref[...], preferred_element_type=F32)
    y += jnp.dot(b_ref[...], wb_ref[...], preferred_element_type=F32)
    o_ref[...] = x_ref[...] + g_ref[0] * y


def out_proj_residual(x, a, b, w, gate, rows_per_mod):
    m, d = x.shape
    ka, kb = a.shape[1], b.shape[1]
    tm = _tile(math.gcd(m, rows_per_mod), 512)
    return pl.pallas_call(
        _outproj_kernel,
        out_shape=jax.ShapeDtypeStruct((m, d), F32),
        grid=(m // tm,),
        in_specs=[pl.BlockSpec((tm, ka), lambda i: (i, 0)),
                  pl.BlockSpec((tm, kb), lambda i: (i, 0)),
                  pl.BlockSpec((ka, d), lambda i: (0, 0)),
                  pl.BlockSpec((kb, d), lambda i: (0, 0)),
                  pl.BlockSpec((tm, d), lambda i: (i, 0)),
                  pl.BlockSpec((1, 1, d), lambda i: (i * tm // rows_per_mod, 0, 0))],
        out_specs=pl.BlockSpec((tm, d), lambda i: (i, 0)),
        compiler_params=_params("parallel"),
        name="out_proj_residual",
    )(a, b, w[:ka], w[ka:], x, gate)


def _ffn_kernel(x_ref, xp_ref, xn_ref, g_ref, sh_ref, sc_ref, gate_ref, wg_ref, wu_ref, cw_ref, cb_ref, wd_ref,
                *rest, tiles_per_seq, final_norm):
    if final_norm:
        fg_ref, o_ref, h_ref, hp_ref, hn_ref, act_ref = rest
    else:
        o_ref, h_ref, hp_ref, hn_ref, act_ref = rest
    j = pl.program_id(1)
    n_chunks = pl.num_programs(1) - 1
    i = pl.program_id(0) % tiles_per_seq
    has_prev = (i != 0).astype(F32)
    has_next = (i != tiles_per_seq - 1).astype(F32)

    def build_activation(slot):
        wg = wg_ref[...]
        gm = jnp.dot(h_ref[...], wg, preferred_element_type=F32)
        up = jnp.dot(h_ref[...], wu_ref[...], preferred_element_type=F32)
        g_prev = jnp.dot(hp_ref[...], wg, preferred_element_type=F32)[F32_SUBLANES - 1:, :] * has_prev
        g_next = jnp.dot(hn_ref[...], wg, preferred_element_type=F32)[:1, :] * has_next
        g_up, g_dn = _shift_rows(gm, g_prev, g_next)
        cw = cw_ref[...]
        conv = g_up * cw[0:1] + gm * cw[1:2] + g_dn * cw[2:3] + cb_ref[...]
        act_ref[slot] = (conv * jax.nn.sigmoid(conv) * up).astype(BF16)

    def down(slot):
        return jnp.dot(act_ref[slot], wd_ref[...], preferred_element_type=F32)

    @pl.when(j == 0)
    def _():
        g, sh, sc = g_ref[...], sh_ref[0], sc_ref[0]
        h_ref[...] = _norm_mod(x_ref[...], g, sh, sc).astype(BF16)
        hp_ref[...] = _norm_mod(xp_ref[...], g, sh, sc).astype(BF16)
        hn_ref[...] = _norm_mod(xn_ref[...], g, sh, sc).astype(BF16)
        o_ref[...] = jnp.zeros(o_ref.shape, F32)
        build_activation(0)

    @pl.when((j > 0) & (j < n_chunks))
    def _():
        o_ref[...] += down((j - 1) % 2)
        build_activation(j % 2)

    @pl.when(j == n_chunks)
    def _():
        y = x_ref[...] + gate_ref[0] * (o_ref[...] + down((j - 1) % 2))
        if final_norm:
            y = y * lax.rsqrt(jnp.mean(y * y, axis=-1, keepdims=True) + EPS) * fg_ref[...]
        o_ref[...] = y


def conv_glu_residual(x, gain, shift, scale, gate, w_gate, w_up, conv_w, conv_b, w_down, seq, rows_per_mod,
                      final_gain=None):
    m, d = x.shape
    f = w_gate.shape[1]
    tm = _tile(math.gcd(seq, rows_per_mod), FFN_TM)
    tf = _tile(f, FFN_TF)
    n_chunks = f // tf
    hb = F32_SUBLANES
    n_halo = m // hb
    mod_spec = pl.BlockSpec((1, 1, d), lambda i, j: (i * tm // rows_per_mod, 0, 0))
    row_spec = pl.BlockSpec((1, d), lambda i, j: (0, 0))

    def this_chunk(i, j):
        return (0, jnp.minimum(j, n_chunks - 1))

    in_specs = [pl.BlockSpec((tm, d), lambda i, j: (i, 0)),
                pl.BlockSpec((hb, d), lambda i, j: (jnp.maximum(i * (tm // hb) - 1, 0), 0)),
                pl.BlockSpec((hb, d), lambda i, j: (jnp.minimum((i + 1) * (tm // hb), n_halo - 1), 0)),
                row_spec, mod_spec, mod_spec, mod_spec,
                pl.BlockSpec((d, tf), this_chunk),
                pl.BlockSpec((d, tf), this_chunk),
                pl.BlockSpec((3, tf), this_chunk),
                pl.BlockSpec((1, tf), this_chunk),
                pl.BlockSpec((tf, d), lambda i, j: (jnp.maximum(j - 1, 0), 0))]
    args = [x, x, x, gain.reshape(1, d), shift, scale, gate, w_gate, w_up,
            conv_w.astype(F32), conv_b.reshape(1, f).astype(F32), w_down]
    if final_gain is not None:
        in_specs.append(row_spec)
        args.append(final_gain.reshape(1, d).astype(F32))
    kern = functools.partial(_ffn_kernel, tiles_per_seq=seq // tm, final_norm=final_gain is not None)
    return pl.pallas_call(
        kern,
        out_shape=jax.ShapeDtypeStruct((m, d), F32),
        grid=(m // tm, n_chunks + 1),
        in_specs=in_specs,
        out_specs=pl.BlockSpec((tm, d), lambda i, j: (i, 0)),
        scratch_shapes=[pltpu.VMEM((tm, d), BF16), pltpu.VMEM((hb, d), BF16), pltpu.VMEM((hb, d), BF16),
                        pltpu.VMEM((2, tm, tf), BF16)],
        compiler_params=_params("parallel", "arbitrary"),
        name="conv_glu_residual",
    )(*args)


def kernel(x, c, ctx, c_ctx, ada_w, ada_b, norm_g, ab_w_in, ab_w_out, na_rpb, gqa_q_gain, gqa_k_gain, cd_w_in,
           cd_w_out, sconv_w, diff_lq1, diff_lk1, diff_lq2, diff_lk2, diff_subln_g, ffn_w_gate, ffn_w_up,
           ffn_conv_w, ffn_conv_b, ffn_w_down, final_g):
    batch, seq, d = x.shape
    ctx_len = ctx.shape[1]
    depth = ada_w.shape[0]
    assert depth == 2 and batch < F32_SUBLANES
    a_w = na_rpb.shape[1] * HEAD_DIM
    b_w = ab_w_out.shape[1] - a_w
    b_kv_w = (ab_w_in.shape[2] - 3 * a_w - b_w) // 2
    n_kv = b_kv_w // HEAD_DIM
    group = b_w // b_kv_w
    ab_kv = a_w + b_w
    c_w = sconv_w.shape[2]
    d_vw = cd_w_out.shape[1] - c_w
    d_heads = d_vw // (2 * HEAD_DIM)
    cd_kv = 3 * c_w + d_vw
    assert c_w == d_vw and cd_w_in.shape[2] == cd_kv + 2 * d_vw
    scale = HEAD_DIM ** -0.5
    ml, mc = batch * seq, batch * ctx_len

    cvec = jnp.zeros((F32_SUBLANES, d), F32).at[:batch].set(c).at[batch].set(c_ctx)
    mods = ada_vectors(cvec, ada_w, ada_b).reshape(depth, F32_SUBLANES, 6, 1, d)
    rope = rope_tables(seq)
    x = x.reshape(ml, d)
    h_ctx = ctx.reshape(mc, d)

    for layer in range(depth):
        last = layer == depth - 1
        i = layer // 2
        lat = [mods[layer, :batch, t] for t in range(6)]
        cm = [mods[layer, batch:batch + 1, t] for t in range(6)]
        if layer % 2 == 0:
            w_in = ab_w_in[i].astype(BF16)
            p = mod_matmul(x, norm_g[layer, 0], lat[0], lat[1], w_in, seq)
            pc = mod_matmul(h_ctx, norm_g[layer, 0], cm[0], cm[1], w_in, mc)
            bq = head_prep(p, a_w, b_w, gain=gqa_q_gain[i], rope=rope, scale=scale * LOG2E, seq=seq)
            bk = head_prep(p, ab_kv + 2 * a_w, b_kv_w, gain=gqa_k_gain[i], rope=rope, seq=seq)
            bck = head_prep(pc, ab_kv + 2 * a_w, b_kv_w, gain=gqa_k_gain[i])
            v_col = ab_kv + 2 * a_w + b_kv_w
            pc3 = pc.reshape(batch, ctx_len, pc.shape[1])
            k_all = jnp.concatenate([bck.reshape(batch, ctx_len, b_kv_w), bk.reshape(batch, seq, b_kv_w)], axis=1)
            vc_ext = with_ones_column(pc3[:, :, v_col:])
            v_all = jnp.concatenate([vc_ext, with_ones_column(p[:, v_col:].reshape(batch, seq, b_kv_w))], axis=1)
            o_a = neighbourhood_attention(p, pc, na_rpb[i], batch, seq, ctx_len, a_w,
                                          ab_kv, ab_kv + a_w, ab_kv, ab_kv + a_w)
            o_b = flash_gqa(bq, k_all, v_all, batch, n_kv, group)
            w_out = ab_w_out[i].astype(BF16)
            x = out_proj_residual(x, o_a, o_b, w_out, lat[2], seq)
            if not last:
                acq = head_prep(pc, 0, a_w, scale=scale * LOG2E)
                bcq = head_prep(pc, a_w, b_w, gain=gqa_q_gain[i], scale=scale * LOG2E)
                oc_a = flash_gqa(acq, pc3, with_ones_column(pc3[:, :, ab_kv + a_w:ab_kv + 2 * a_w]), batch,
                                 a_w // HEAD_DIM, 1, k_col=ab_kv)
                oc_b = flash_gqa(bcq, bck.reshape(batch, ctx_len, b_kv_w), vc_ext, batch, n_kv, group)
                h_ctx = out_proj_residual(h_ctx, oc_a, oc_b, w_out, cm[2], mc)
        else:
            assert last, "a non-final short-conv / differential layer is not needed at this depth"
            lam_init = 0.8 - 0.6 * math.exp(-0.3 * layer)
            w_in = cd_w_in[i].astype(BF16)
            p = mod_matmul(x, norm_g[layer, 0], lat[0], lat[1], w_in, seq)
            pc = mod_matmul(h_ctx, norm_g[layer, 0], cm[0], cm[1], w_in[:, cd_kv:], mc)
            dq = head_prep(p, 3 * c_w, d_vw, rope=rope, scale=scale * LOG2E, seq=seq)
            dk = head_prep(p, cd_kv, d_vw, rope=rope, seq=seq)
            k_all = jnp.concatenate([pc[:, :d_vw].reshape(batch, ctx_len, d_vw), dk.reshape(batch, seq, d_vw)], axis=1)
            v_all = jnp.concatenate([pc[:, d_vw:].reshape(batch, ctx_len, d_vw),
                                     p[:, cd_kv + d_vw:].reshape(batch, seq, d_vw)], axis=1)
            y_c = short_conv(p, sconv_w[i], seq)
            o_d = flash_diff(dq, k_all, v_all, (diff_lq1[i], diff_lk1[i], diff_lq2[i], diff_lk2[i]),
                             diff_subln_g[i], lam_init, batch, d_heads)
            x = out_proj_residual(x, y_c, o_d, cd_w_out[i].astype(BF16), lat[2], seq)
        ffn_w = (ffn_w_gate[layer].astype(BF16), ffn_w_up[layer].astype(BF16), ffn_conv_w[layer],
                 ffn_conv_b[layer], ffn_w_down[layer].astype(BF16))
        x = conv_glu_residual(x, norm_g[layer, 1], lat[3], lat[4], lat[5], *ffn_w, seq, seq,
                              final_gain=final_g if last else None)
        if not last:
            h_ctx = conv_glu_residual(h_ctx, norm_g[layer, 1], cm[3], cm[4], cm[5], *ffn_w, ctx_len, mc)
    return x.reshape(batch, seq, d)
```

```python
import functools
import math

import numpy as np
import jax
import jax.numpy as jnp
from jax import lax
from jax.experimental import pallas as pl
from jax.experimental.pallas import tpu as pltpu

HEAD_DIM = 128
GRID_W = 64
ROPE_THETA = 10000.0
EPS = 1e-6
F32 = jnp.float32
BF16 = jnp.bfloat16
NEG = -1e30
VMEM_LIMIT_BYTES = 60 * 1024 * 1024
BF16_SUBLANES = 16
F32_SUBLANES = 8
NT_DIMS = (((1,), (1,)), ((), ()))
FLASH_TQ = 1024
FLASH_TK = 3328
FLASH_ITEM_ROWS = 512
FFN_TM = 1024
FFN_TF = 512
LOG2E = math.log2(math.e)


def _params(*sem):
    return pltpu.CompilerParams(dimension_semantics=sem, vmem_limit_bytes=VMEM_LIMIT_BYTES)


def _tile(n, want):
    t = min(n, want)
    while n % t:
        t -= 1
    return t


def _norm_mod(x, g, sh, sc):
    y = x * lax.rsqrt(jnp.mean(x * x, axis=-1, keepdims=True) + EPS)
    return (y * g) * (1.0 + sc) + sh


def _ada_kernel(c_ref, w_ref, b_ref, o_ref):
    c = c_ref[...]
    a = c * jax.nn.sigmoid(c)
    a_hi = a.astype(BF16)
    a_lo = (a - a_hi.astype(F32)).astype(BF16)
    w = w_ref[0]
    w_hi = w.astype(BF16)
    w_lo = (w - w_hi.astype(F32)).astype(BF16)
    acc = jnp.dot(a_hi, w_hi, preferred_element_type=F32)
    acc += jnp.dot(a_lo, w_hi, preferred_element_type=F32)
    acc += jnp.dot(a_hi, w_lo, preferred_element_type=F32)
    o_ref[0] = acc + b_ref[0]


def ada_vectors(cvec, ada_w, ada_b):
    depth, d, n = ada_w.shape
    tn = _tile(n, 512)
    return pl.pallas_call(
        _ada_kernel,
        out_shape=jax.ShapeDtypeStruct((depth, F32_SUBLANES, n), F32),
        grid=(depth, n // tn),
        in_specs=[pl.BlockSpec((F32_SUBLANES, d), lambda l, j: (0, 0)),
                  pl.BlockSpec((1, d, tn), lambda l, j: (l, 0, j)),
                  pl.BlockSpec((1, 1, tn), lambda l, j: (l, 0, j))],
        out_specs=pl.BlockSpec((1, F32_SUBLANES, tn), lambda l, j: (l, 0, j)),
        compiler_params=_params("parallel", "parallel"),
        name="ada_vectors",
    )(cvec, ada_w, ada_b.reshape(depth, 1, n))


def _modmm_kernel(x_ref, g_ref, sh_ref, sc_ref, w_ref, o_ref, h_ref):
    @pl.when(pl.program_id(1) == 0)
    def _():
        h_ref[...] = _norm_mod(x_ref[...], g_ref[...], sh_ref[0], sc_ref[0]).astype(BF16)

    o_ref[...] = jnp.dot(h_ref[...], w_ref[...], preferred_element_type=F32).astype(o_ref.dtype)


def mod_matmul(x, gain, shift, scale, w, rows_per_mod):
    m, d = x.shape
    n = w.shape[1]
    tm = _tile(math.gcd(m, rows_per_mod), 1024)
    tn = next(t for t in (1536, 1024, 768, 512, 256, HEAD_DIM) if n % t == 0)
    mod_spec = pl.BlockSpec((1, 1, d), lambda i, j: (i * tm // rows_per_mod, 0, 0))
    return pl.pallas_call(
        _modmm_kernel,
        out_shape=jax.ShapeDtypeStruct((m, n), BF16),
        grid=(m // tm, n // tn),
        in_specs=[pl.BlockSpec((tm, d), lambda i, j: (i, 0)),
                  pl.BlockSpec((1, d), lambda i, j: (0, 0)),
                  mod_spec, mod_spec,
                  pl.BlockSpec((d, tn), lambda i, j: (0, j))],
        out_specs=pl.BlockSpec((tm, tn), lambda i, j: (i, j)),
        scratch_shapes=[pltpu.VMEM((tm, d), BF16)],
        compiler_params=_params("parallel", "arbitrary"),
        name="mod_matmul",
    )(x, gain.reshape(1, d), shift, scale, w)


def _prep_kernel(*refs, n_heads, use_gain, use_rope, scale):
    refs = list(refs)
    x_ref = refs.pop(0)
    gain = refs.pop(0)[...] if use_gain else None
    if use_rope:
        cos, sin_a, sin_b = refs.pop(0)[...], refs.pop(0)[...], refs.pop(0)[...]
    o_ref = refs.pop(0)
    for h in range(n_heads):
        sl = slice(h * HEAD_DIM, (h + 1) * HEAD_DIM)
        x = x_ref[:, sl].astype(F32)
        if use_gain:
            x = x * lax.rsqrt(jnp.mean(x * x, axis=-1, keepdims=True) + EPS) * gain
        if use_rope:
            x = (x * cos + pltpu.roll(x, HEAD_DIM - HEAD_DIM // 4, 1) * sin_a
                 + pltpu.roll(x, HEAD_DIM // 4, 1) * sin_b)
        if scale != 1.0:
            x = x * scale
        o_ref[:, sl] = x.astype(o_ref.dtype)


def head_prep(p, col_start, width, gain=None, rope=None, scale=1.0, seq=None):
    m = p.shape[0]
    assert col_start % width == 0
    tm = _tile(seq if rope is not None else m, 512)
    args = [p]
    in_specs = [pl.BlockSpec((tm, width), lambda i: (i, col_start // width))]
    if gain is not None:
        args.append(gain.reshape(1, HEAD_DIM).astype(F32))
        in_specs.append(pl.BlockSpec((1, HEAD_DIM), lambda i: (0, 0)))
    if rope is not None:
        n_seq_tiles = seq // tm
        args.extend(rope)
        in_specs.extend([pl.BlockSpec((tm, HEAD_DIM), lambda i: (i % n_seq_tiles, 0))] * 3)
    kern = functools.partial(_prep_kernel, n_heads=width // HEAD_DIM, use_gain=gain is not None,
                             use_rope=rope is not None, scale=scale)
    return pl.pallas_call(
        kern,
        out_shape=jax.ShapeDtypeStruct((m, width), BF16),
        grid=(m // tm,),
        in_specs=in_specs,
        out_specs=pl.BlockSpec((tm, width), lambda i: (i, 0)),
        compiler_params=_params("parallel"),
        name="head_prep",
    )(*args)


def rope_tables(seq):
    t = jnp.arange(seq)
    axis_dim = HEAD_DIM // 2
    inv = 1.0 / (ROPE_THETA ** (jnp.arange(0, axis_dim, 2, dtype=F32) / axis_dim))
    ang_r = (t // GRID_W).astype(F32)[:, None] * inv
    ang_c = (t % GRID_W).astype(F32)[:, None] * inv
    ang = jnp.concatenate([ang_r, ang_r, ang_c, ang_c], axis=-1)
    cos, sin = jnp.cos(ang), jnp.sin(ang)
    first_half = (np.arange(HEAD_DIM) % (HEAD_DIM // 2)) < HEAD_DIM // 4
    sin_a = jnp.where(first_half[None], -sin, 0.0)
    sin_b = jnp.where(first_half[None], 0.0, sin)
    return cos, sin_a, sin_b


def _na_kernel(q_ref, k0_ref, k1_ref, k2_ref, v0_ref, v1_ref, v2_ref, kc_ref, vc_ref, bias_ref, o_ref,
               *, n_heads, scale):
    k_refs = (k0_ref, k1_ref, k2_ref)
    v_refs = (v0_ref, v1_ref, v2_ref)
    def head(h):
        return slice(h * HEAD_DIM, (h + 1) * HEAD_DIM)

    def scores(h):
        q = (q_ref[:, head(h)].astype(F32) * (scale * LOG2E)).astype(BF16)
        s = [lax.dot_general(q, k_refs[j][:, head(h)], NT_DIMS, preferred_element_type=F32) for j in range(3)]
        s = jnp.concatenate(s, axis=1) + bias_ref[0, h]
        return jnp.concatenate([s, lax.dot_general(q, kc_ref[:, head(h)], NT_DIMS, preferred_element_type=F32)], axis=1)

    def softmax(h, s):
        p = jnp.exp2(s - jnp.max(s, axis=1, keepdims=True))
        return p.astype(BF16), jnp.sum(p, axis=1, keepdims=True)

    def accumulate(h, p, l):
        v = jnp.concatenate([v_refs[j][:, head(h)] for j in range(3)] + [vc_ref[:, head(h)]], axis=0)
        o_ref[:, head(h)] = (jnp.dot(p, v, preferred_element_type=F32) / l).astype(o_ref.dtype)

    _pipelined(n_heads, scores, softmax, accumulate)


def _na_bias_table(rpb, rows, na_rows, na_cols):
    rq_rows = na_rows // 2
    n_blk = rows // rq_rows
    n_heads = rpb.shape[0]
    w = GRID_W
    assert na_cols <= w
    v = jnp.concatenate([rpb[:, :, na_cols - 1:], jnp.zeros((n_heads, rpb.shape[1], 2 * w - rpb.shape[2]), rpb.dtype),
                         rpb[:, :, :na_cols - 1]], axis=-1).astype(F32)
    toep = jnp.tile(v, (1, 1, w + 1))[:, :, :w * (2 * w - 1)].reshape(n_heads, rpb.shape[1], w, 2 * w - 1)[..., :w]
    tabs = []
    for qb, kb in ((0, 0), (1, 0), (n_blk - 1, n_blk - 3)):
        rq = np.repeat(qb * rq_rows + np.arange(rq_rows), w)[:, None]
        cq = np.tile(np.arange(w), rq_rows)[:, None]
        rk = np.repeat(kb * rq_rows + np.arange(3 * rq_rows), w)[None, :]
        ck = np.tile(np.arange(w), 3 * rq_rows)[None, :]
        r0 = np.clip(rq - na_rows // 2, 0, rows - na_rows)
        c0 = np.clip(cq - na_cols // 2, 0, w - na_cols)
        valid = (rk >= r0) & (rk < r0 + na_rows) & (ck >= c0) & (ck < c0 + na_cols)
        row_blocks = []
        for i in range(rq_rows):
            dr = [int(np.clip((kb - qb) * rq_rows + j - i + na_rows - 1, 0, 2 * na_rows - 2)) for j in range(3 * rq_rows)]
            row_blocks.append(jnp.concatenate([toep[:, d] for d in dr], axis=-1))
        tabs.append(jnp.where(valid[None], jnp.concatenate(row_blocks, axis=1) * LOG2E, NEG))
    return jnp.stack(tabs)


def neighbourhood_attention(p_lat, p_ctx, rpb, batch, seq, ctx_len, a_w, k_col, v_col, ck_col, cv_col):
    n_heads = a_w // HEAD_DIM
    na_rows, na_cols = (rpb.shape[1] + 1) // 2, (rpb.shape[2] + 1) // 2
    rows = seq // GRID_W
    rq_rows = na_rows // 2
    tq = rq_rows * GRID_W
    n_blk = rows // rq_rows
    assert rows >= na_rows and rows % rq_rows == 0 and n_blk >= 3 and na_rows % 2 == 0
    assert k_col % a_w == 0 and v_col % a_w == 0 and ck_col % a_w == 0 and cv_col % a_w == 0
    bias = _na_bias_table(rpb, rows, na_rows, na_cols)

    def kv_spec(col, j):
        return pl.BlockSpec((tq, a_w), lambda b, r: (b * n_blk + jnp.clip(r - 1, 0, n_blk - 3) + j, col // a_w))

    kern = functools.partial(_na_kernel, n_heads=n_heads, scale=HEAD_DIM ** -0.5)
    return pl.pallas_call(
        kern,
        out_shape=jax.ShapeDtypeStruct((batch * seq, a_w), BF16),
        grid=(batch, n_blk),
        in_specs=[pl.BlockSpec((tq, a_w), lambda b, r: (b * n_blk + r, 0)),
                  kv_spec(k_col, 0), kv_spec(k_col, 1), kv_spec(k_col, 2),
                  kv_spec(v_col, 0), kv_spec(v_col, 1), kv_spec(v_col, 2),
                  pl.BlockSpec((ctx_len, a_w), lambda b, r: (b, ck_col // a_w)),
                  pl.BlockSpec((ctx_len, a_w), lambda b, r: (b, cv_col // a_w)),
                  pl.BlockSpec((1, n_heads, tq, 3 * tq),
                               lambda b, r: (jnp.where(r == 0, 0, jnp.where(r == n_blk - 1, 2, 1)), 0, 0, 0))],
        out_specs=pl.BlockSpec((tq, a_w), lambda b, r: (b * n_blk + r, 0)),
        compiler_params=_params("parallel", "arbitrary"),
        name="neighbourhood_attention",
    )(p_lat, p_lat, p_lat, p_lat, p_lat, p_lat, p_lat, p_ctx, p_ctx, bias)


def _pipelined(n_items, scores, softmax, accumulate):
    ahead, lag = 2, 1
    s = {j: scores(j) for j in range(min(ahead, n_items))}
    pending = {}
    for i in range(n_items):
        if i + ahead < n_items:
            s[i + ahead] = scores(i + ahead)
        pending[i] = softmax(i, s.pop(i))
        if i >= lag:
            accumulate(i - lag, *pending.pop(i - lag))
    for i in sorted(pending):
        accumulate(i, *pending[i])


def _flash_kernel(q_ref, k_ref, v_ref, o_ref, m_ref, acc_ref, *, group, rows_per_item):
    kv = pl.program_id(3)

    @pl.when(kv == 0)
    def _():
        m_ref[...] = jnp.full(m_ref.shape, NEG, F32)
        acc_ref[...] = jnp.zeros(acc_ref.shape, F32)

    k = k_ref[0]
    v = v_ref[0]
    rc = rows_per_item
    items = [(r, c) for r in range(group) for c in range(q_ref.shape[0] // rc)]

    def scores(i):
        r, c = items[i]
        q = q_ref[c * rc:(c + 1) * rc, r * HEAD_DIM:(r + 1) * HEAD_DIM]
        return lax.dot_general(q, k, NT_DIMS, preferred_element_type=F32)

    def softmax(i, s):
        r, c = items[i]
        rows = slice(c * rc, (c + 1) * rc)
        m_prev = m_ref[r, rows]
        m_new = jnp.maximum(m_prev, jnp.max(s, axis=1, keepdims=True))
        m_ref[r, rows] = m_new
        return jnp.exp2((s - m_new[:, :1]).astype(BF16)), jnp.exp2(m_prev - m_new)

    def accumulate(i, p, alpha):
        r, c = items[i]
        rows = slice(c * rc, (c + 1) * rc)
        acc_ref[r, rows] = (jnp.concatenate([alpha, alpha], axis=1) * acc_ref[r, rows]
                            + jnp.dot(p, v, preferred_element_type=F32))

    _pipelined(len(items), scores, softmax, accumulate)

    @pl.when(kv == pl.num_programs(3) - 1)
    def _():
        for r in range(group):
            a = acc_ref[r]
            o_ref[:, r * HEAD_DIM:(r + 1) * HEAD_DIM] = (a[:, :HEAD_DIM] / a[:, HEAD_DIM:HEAD_DIM + 1]).astype(o_ref.dtype)


def with_ones_column(v):
    b, sk, w = v.shape
    g = w // HEAD_DIM
    tail = jnp.zeros((b, sk, g, HEAD_DIM), v.dtype).at[..., 0].set(1)
    return jnp.concatenate([v.reshape(b, sk, g, HEAD_DIM), tail], axis=-1).reshape(b, sk, 2 * w)


def flash_gqa(q, k, v_ext, batch, n_kv_heads, group, k_col=0):
    sq = q.shape[0] // batch
    sk = k.shape[1]
    tq = _tile(sq, FLASH_TQ)
    tk = _tile(sk, FLASH_TK)
    rc = _tile(tq, FLASH_ITEM_ROWS)
    assert tk % BF16_SUBLANES == 0 and k_col % HEAD_DIM == 0
    nq = sq // tq
    kern = functools.partial(_flash_kernel, group=group, rows_per_item=rc)
    return pl.pallas_call(
        kern,
        out_shape=jax.ShapeDtypeStruct(q.shape, BF16),
        grid=(batch, n_kv_heads, nq, sk // tk),
        in_specs=[pl.BlockSpec((tq, group * HEAD_DIM), lambda b, g, i, j: (b * nq + i, g)),
                  pl.BlockSpec((1, tk, HEAD_DIM), lambda b, g, i, j: (b, j, k_col // HEAD_DIM + g)),
                  pl.BlockSpec((1, tk, 2 * HEAD_DIM), lambda b, g, i, j: (b, j, g))],
        out_specs=pl.BlockSpec((tq, group * HEAD_DIM), lambda b, g, i, j: (b * nq + i, g)),
        scratch_shapes=[pltpu.VMEM((group, tq, HEAD_DIM), F32),
                        pltpu.VMEM((group, tq, 2 * HEAD_DIM), F32)],
        compiler_params=_params("parallel", "parallel", "parallel", "arbitrary"),
        name="flash_gqa",
    )(q, k, v_ext)


def _diff_kernel(q_ref, k_ref, v_ref, lq1_ref, lk1_ref, lq2_ref, lk2_ref, g_ref, o_ref,
                 m_ref, l_ref, acc_ref, *, lam_init, rows_per_item):
    kv = pl.program_id(3)

    @pl.when(kv == 0)
    def _():
        m_ref[...] = jnp.full(m_ref.shape, NEG, F32)
        l_ref[...] = jnp.zeros(l_ref.shape, F32)
        acc_ref[...] = jnp.zeros(acc_ref.shape, F32)

    v = v_ref[0]
    rc = rows_per_item
    items = [(c, t) for c in range(2) for t in range(q_ref.shape[0] // rc)]

    def scores(i):
        c, t = items[i]
        sl = slice(c * HEAD_DIM, (c + 1) * HEAD_DIM)
        return lax.dot_general(q_ref[t * rc:(t + 1) * rc, sl], k_ref[0, :, sl], NT_DIMS, preferred_element_type=F32)

    def softmax(i, s):
        c, t = items[i]
        rows = slice(t * rc, (t + 1) * rc)
        m_prev = m_ref[c, rows]
        m_new = jnp.maximum(m_prev, jnp.max(s, axis=1, keepdims=True))
        alpha = jnp.exp2(m_prev - m_new)
        p = jnp.exp2(s - m_new[:, :1])
        part = p[:, :HEAD_DIM]
        for j in range(1, p.shape[1] // HEAD_DIM):
            part = part + p[:, j * HEAD_DIM:(j + 1) * HEAD_DIM]
        l_ref[c, rows] = alpha * l_ref[c, rows] + part
        m_ref[c, rows] = m_new
        return p.astype(BF16), alpha

    def accumulate(i, p, alpha):
        c, t = items[i]
        rows = slice(t * rc, (t + 1) * rc)
        acc_ref[c, rows] = (jnp.concatenate([alpha, alpha], axis=1) * acc_ref[c, rows]
                            + jnp.dot(p, v, preferred_element_type=F32))

    _pipelined(len(items), scores, softmax, accumulate)

    @pl.when(kv == pl.num_programs(3) - 1)
    def _():
        lam = (jnp.exp(jnp.sum(lq1_ref[...] * lk1_ref[...], axis=-1, keepdims=True))
               - jnp.exp(jnp.sum(lq2_ref[...] * lk2_ref[...], axis=-1, keepdims=True)) + lam_init)
        l1 = jnp.sum(l_ref[0], axis=-1, keepdims=True)
        l2 = jnp.sum(l_ref[1], axis=-1, keepdims=True)
        o = acc_ref[0] / l1 - lam * (acc_ref[1] / l2)
        o = o * lax.rsqrt(jnp.mean(o * o, axis=-1, keepdims=True) + EPS) * g_ref[...]
        o_ref[...] = (o * (1.0 - lam_init)).astype(o_ref.dtype)


def flash_diff(q, k, v, lam_vecs, subln_g, lam_init, batch, n_heads):
    sq = q.shape[0] // batch
    sk = k.shape[1]
    tq = _tile(sq, FLASH_TQ)
    tk = _tile(sk, FLASH_TK)
    assert tk % HEAD_DIM == 0
    nq = sq // tq
    w = 2 * HEAD_DIM
    vec_spec = pl.BlockSpec((1, HEAD_DIM), lambda b, h, i, j: (0, 0))
    kern = functools.partial(_diff_kernel, lam_init=lam_init, rows_per_item=_tile(tq, FLASH_ITEM_ROWS))
    return pl.pallas_call(
        kern,
        out_shape=jax.ShapeDtypeStruct(q.shape, BF16),
        grid=(batch, n_heads, nq, sk // tk),
        in_specs=[pl.BlockSpec((tq, w), lambda b, h, i, j: (b * nq + i, h)),
                  pl.BlockSpec((1, tk, w), lambda b, h, i, j: (b, j, h)),
                  pl.BlockSpec((1, tk, w), lambda b, h, i, j: (b, j, h)),
                  vec_spec, vec_spec, vec_spec, vec_spec,
                  pl.BlockSpec((1, w), lambda b, h, i, j: (0, 0))],
        out_specs=pl.BlockSpec((tq, w), lambda b, h, i, j: (b * nq + i, h)),
        scratch_shapes=[pltpu.VMEM((2, tq, HEAD_DIM), F32),
                        pltpu.VMEM((2, tq, HEAD_DIM), F32),
                        pltpu.VMEM((2, tq, w), F32)],
        compiler_params=_params("parallel", "parallel", "parallel", "arbitrary"),
        name="flash_diff",
    )(q, k, v, *[t.reshape(1, HEAD_DIM).astype(F32) for t in lam_vecs], subln_g.reshape(1, w).astype(F32))


def _shift_rows(x, first_row, last_row):
    n = x.shape[0]
    row = lax.broadcasted_iota(jnp.int32, x.shape, 0)
    up = jnp.where(row == 0, first_row, pltpu.roll(x, 1, 0))
    dn = jnp.where(row == n - 1, last_row, pltpu.roll(x, n - 1, 0))
    return up, dn


def _sconv_kernel(u_ref, gb_ref, gc_ref, up_ref, gcp_ref, un_ref, gcn_ref, w_ref, o_ref, *, tiles_per_seq):
    i = pl.program_id(0) % tiles_per_seq
    has_prev = (i != 0).astype(F32)
    has_next = (i != tiles_per_seq - 1).astype(F32)
    x = gc_ref[...].astype(F32) * u_ref[...].astype(F32)
    prev = (gcp_ref[...].astype(F32) * up_ref[...].astype(F32))[BF16_SUBLANES - 1:, :] * has_prev
    nxt = (gcn_ref[...].astype(F32) * un_ref[...].astype(F32))[:1, :] * has_next
    x_up, x_dn = _shift_rows(x, prev, nxt)
    w = w_ref[...]
    y = x_up * w[0:1] + x * w[1:2] + x_dn * w[2:3]
    o_ref[...] = (gb_ref[...].astype(F32) * y).astype(o_ref.dtype)


def short_conv(p, conv_w, seq):
    m = p.shape[0]
    c = conv_w.shape[1]
    tm = _tile(seq, 512)
    hb = BF16_SUBLANES
    n_halo = m // hb
    kern = functools.partial(_sconv_kernel, tiles_per_seq=seq // tm)

    def prev_spec(col):
        return pl.BlockSpec((hb, c), lambda i: (jnp.maximum(i * (tm // hb) - 1, 0), col))

    def next_spec(col):
        return pl.BlockSpec((hb, c), lambda i: (jnp.minimum((i + 1) * (tm // hb), n_halo - 1), col))

    return pl.pallas_call(
        kern,
        out_shape=jax.ShapeDtypeStruct((m, c), BF16),
        grid=(m // tm,),
        in_specs=[pl.BlockSpec((tm, c), lambda i: (i, 0)),
                  pl.BlockSpec((tm, c), lambda i: (i, 1)),
                  pl.BlockSpec((tm, c), lambda i: (i, 2)),
                  prev_spec(0), prev_spec(2), next_spec(0), next_spec(2),
                  pl.BlockSpec((3, c), lambda i: (0, 0))],
        out_specs=pl.BlockSpec((tm, c), lambda i: (i, 0)),
        compiler_params=_params("parallel"),
        name="short_conv",
    )(p, p, p, p, p, p, p, conv_w.astype(F32))


def _outproj_kernel(a_ref, b_ref, wa_ref, wb_ref, x_ref, g_ref, o_ref):
    y = jnp.dot(a_ref[...], wa_ref[...], preferred_element_type=F32)
    y += jnp.dot(b_ref[...], wb_ref[...], preferred_element_type=F32)
    o_ref[...] = x_ref[...] + g_ref[0] * y


def out_proj_residual(x, a, b, w, gate, rows_per_mod):
    m, d = x.shape
    ka, kb = a.shape[1], b.shape[1]
    tm = _tile(math.gcd(m, rows_per_mod), 512)
    return pl.pallas_call(
        _outproj_kernel,
        out_shape=jax.ShapeDtypeStruct((m, d), F32),
        grid=(m // tm,),
        in_specs=[pl.BlockSpec((tm, ka), lambda i: (i, 0)),
                  pl.BlockSpec((tm, kb), lambda i: (i, 0)),
                  pl.BlockSpec((ka, d), lambda i: (0, 0)),
                  pl.BlockSpec((kb, d), lambda i: (0, 0)),
                  pl.BlockSpec((tm, d), lambda i: (i, 0)),
                  pl.BlockSpec((1, 1, d), lambda i: (i * tm // rows_per_mod, 0, 0))],
        out_specs=pl.BlockSpec((tm, d), lambda i: (i, 0)),
        compiler_params=_params("parallel"),
        name="out_proj_residual",
    )(a, b, w[:ka], w[ka:], x, gate)


def _ffn_kernel(x_ref, xp_ref, xn_ref, g_ref, sh_ref, sc_ref, gate_ref, wg_ref, wu_ref, cw_ref, cb_ref, wd_ref,
                *rest, tiles_per_seq, final_norm):
    if final_norm:
        fg_ref, o_ref, h_ref, hp_ref, hn_ref, act_ref = rest
    else:
        o_ref, h_ref, hp_ref, hn_ref, act_ref = rest
    j = pl.program_id(1)
    n_chunks = pl.num_programs(1) - 1
    i = pl.program_id(0) % tiles_per_seq
    has_prev = (i != 0).astype(F32)
    has_next = (i != tiles_per_seq - 1).astype(F32)

    def build_activation(slot):
        wg = wg_ref[...]
        gm = jnp.dot(h_ref[...], wg, preferred_element_type=F32)
        up = jnp.dot(h_ref[...], wu_ref[...], preferred_element_type=F32)
        g_prev = jnp.dot(hp_ref[...], wg, preferred_element_type=F32)[F32_SUBLANES - 1:, :] * has_prev
        g_next = jnp.dot(hn_ref[...], wg, preferred_element_type=F32)[:1, :] * has_next
        g_up, g_dn = _shift_rows(gm, g_prev, g_next)
        cw = cw_ref[...]
        conv = g_up * cw[0:1] + gm * cw[1:2] + g_dn * cw[2:3] + cb_ref[...]
        act_ref[slot] = (conv * jax.nn.sigmoid(conv) * up).astype(BF16)

    def down(slot):
        return jnp.dot(act_ref[slot], wd_ref[...], preferred_element_type=F32)

    @pl.when(j == 0)
    def _():
        g, sh, sc = g_ref[...], sh_ref[0], sc_ref[0]
        h_ref[...] = _norm_mod(x_ref[...], g, sh, sc).astype(BF16)
        hp_ref[...] = _norm_mod(xp_ref[...], g, sh, sc).astype(BF16)
        hn_ref[...] = _norm_mod(xn_ref[...], g, sh, sc).astype(BF16)
        o_ref[...] = jnp.zeros(o_ref.shape, F32)
        build_activation(0)

    @pl.when((j > 0) & (j < n_chunks))
    def _():
        o_ref[...] += down((j - 1) % 2)
        build_activation(j % 2)

    @pl.when(j == n_chunks)
    def _():
        y = x_ref[...] + gate_ref[0] * (o_ref[...] + down((j - 1) % 2))
        if final_norm:
            y = y * lax.rsqrt(jnp.mean(y * y, axis=-1, keepdims=True) + EPS) * fg_ref[...]
        o_ref[...] = y


def conv_glu_residual(x, gain, shift, scale, gate, w_gate, w_up, conv_w, conv_b, w_down, seq, rows_per_mod,
                      final_gain=None):
    m, d = x.shape
    f = w_gate.shape[1]
    tm = _tile(math.gcd(seq, rows_per_mod), FFN_TM)
    tf = _tile(f, FFN_TF)
    n_chunks = f // tf
    hb = F32_SUBLANES
    n_halo = m // hb
    mod_spec = pl.BlockSpec((1, 1, d), lambda i, j: (i * tm // rows_per_mod, 0, 0))
    row_spec = pl.BlockSpec((1, d), lambda i, j: (0, 0))

    def this_chunk(i, j):
        return (0, jnp.minimum(j, n_chunks - 1))

    in_specs = [pl.BlockSpec((tm, d), lambda i, j: (i, 0)),
                pl.BlockSpec((hb, d), lambda i, j: (jnp.maximum(i * (tm // hb) - 1, 0), 0)),
                pl.BlockSpec((hb, d), lambda i, j: (jnp.minimum((i + 1) * (tm // hb), n_halo - 1), 0)),
                row_spec, mod_spec, mod_spec, mod_spec,
                pl.BlockSpec((d, tf), this_chunk),
                pl.BlockSpec((d, tf), this_chunk),
                pl.BlockSpec((3, tf), this_chunk),
                pl.BlockSpec((1, tf), this_chunk),
                pl.BlockSpec((tf, d), lambda i, j: (jnp.maximum(j - 1, 0), 0))]
    args = [x, x, x, gain.reshape(1, d), shift, scale, gate, w_gate, w_up,
            conv_w.astype(F32), conv_b.reshape(1, f).astype(F32), w_down]
    if final_gain is not None:
        in_specs.append(row_spec)
        args.append(final_gain.reshape(1, d).astype(F32))
    kern = functools.partial(_ffn_kernel, tiles_per_seq=seq // tm, final_norm=final_gain is not None)
    return pl.pallas_call(
        kern,
        out_shape=jax.ShapeDtypeStruct((m, d), F32),
        grid=(m // tm, n_chunks + 1),
        in_specs=in_specs,
        out_specs=pl.BlockSpec((tm, d), lambda i, j: (i, 0)),
        scratch_shapes=[pltpu.VMEM((tm, d), BF16), pltpu.VMEM((hb, d), BF16), pltpu.VMEM((hb, d), BF16),
                        pltpu.VMEM((2, tm, tf), BF16)],
        compiler_params=_params("parallel", "arbitrary"),
        name="conv_glu_residual",
    )(*args)


def kernel(x, c, ctx, c_ctx, ada_w, ada_b, norm_g, ab_w_in, ab_w_out, na_rpb, gqa_q_gain, gqa_k_gain, cd_w_in,
           cd_w_out, sconv_w, diff_lq1, diff_lk1, diff_lq2, diff_lk2, diff_subln_g, ffn_w_gate, ffn_w_up,
           ffn_conv_w, ffn_conv_b, ffn_w_down, final_g):
    batch, seq, d = x.shape
    ctx_len = ctx.shape[1]
    depth = ada_w.shape[0]
    assert depth == 2 and batch < F32_SUBLANES
    a_w = na_rpb.shape[1] * HEAD_DIM
    b_w = ab_w_out.shape[1] - a_w
    b_kv_w = (ab_w_in.shape[2] - 3 * a_w - b_w) // 2
    n_kv = b_kv_w // HEAD_DIM
    group = b_w // b_kv_w
    ab_kv = a_w + b_w
    c_w = sconv_w.shape[2]
    d_vw = cd_w_out.shape[1] - c_w
    d_heads = d_vw // (2 * HEAD_DIM)
    cd_kv = 3 * c_w + d_vw
    assert c_w == d_vw and cd_w_in.shape[2] == cd_kv + 2 * d_vw
    scale = HEAD_DIM ** -0.5
    ml, mc = batch * seq, batch * ctx_len

    cvec = jnp.zeros((F32_SUBLANES, d), F32).at[:batch].set(c).at[batch].set(c_ctx)
    mods = ada_vectors(cvec, ada_w, ada_b).reshape(depth, F32_SUBLANES, 6, 1, d)
    rope = rope_tables(seq)
    x = x.reshape(ml, d)
    h_ctx = ctx.reshape(mc, d)

    for layer in range(depth):
        last = layer == depth - 1
        i = layer // 2
        lat = [mods[layer, :batch, t] for t in range(6)]
        cm = [mods[layer, batch:batch + 1, t] for t in range(6)]
        if layer % 2 == 0:
            w_in = ab_w_in[i].astype(BF16)
            p = mod_matmul(x, norm_g[layer, 0], lat[0], lat[1], w_in, seq)
            pc = mod_matmul(h_ctx, norm_g[layer, 0], cm[0], cm[1], w_in, mc)
            bq = head_prep(p, a_w, b_w, gain=gqa_q_gain[i], rope=rope, scale=scale * LOG2E, seq=seq)
            bk = head_prep(p, ab_kv + 2 * a_w, b_kv_w, gain=gqa_k_gain[i], rope=rope, seq=seq)
            bck = head_prep(pc, ab_kv + 2 * a_w, b_kv_w, gain=gqa_k_gain[i])
            v_col = ab_kv + 2 * a_w + b_kv_w
            pc3 = pc.reshape(batch, ctx_len, pc.shape[1])
            k_all = jnp.concatenate([bck.reshape(batch, ctx_len, b_kv_w), bk.reshape(batch, seq, b_kv_w)], axis=1)
            vc_ext = with_ones_column(pc3[:, :, v_col:])
            v_all = jnp.concatenate([vc_ext, with_ones_column(p[:, v_col:].reshape(batch, seq, b_kv_w))], axis=1)
            o_a = neighbourhood_attention(p, pc, na_rpb[i], batch, seq, ctx_len, a_w,
                                          ab_kv, ab_kv + a_w, ab_kv, ab_kv + a_w)
            o_b = flash_gqa(bq, k_all, v_all, batch, n_kv, group)
            w_out = ab_w_out[i].astype(BF16)
            x = out_proj_residual(x, o_a, o_b, w_out, lat[2], seq)
            if not last:
                acq = head_prep(pc, 0, a_w, scale=scale * LOG2E)
                bcq = head_prep(pc, a_w, b_w, gain=gqa_q_gain[i], scale=scale * LOG2E)
                oc_a = flash_gqa(acq, pc3, with_ones_column(pc3[:, :, ab_kv + a_w:ab_kv + 2 * a_w]), batch,
                                 a_w // HEAD_DIM, 1, k_col=ab_kv)
                oc_b = flash_gqa(bcq, bck.reshape(batch, ctx_len, b_kv_w), vc_ext, batch, n_kv, group)
                h_ctx = out_proj_residual(h_ctx, oc_a, oc_b, w_out, cm[2], mc)
        else:
            assert last, "a non-final short-conv / differential layer is not needed at this depth"
            lam_init = 0.8 - 0.6 * math.exp(-0.3 * layer)
            w_in = cd_w_in[i].astype(BF16)
            p = mod_matmul(x, norm_g[layer, 0], lat[0], lat[1], w_in, seq)
            pc = mod_matmul(h_ctx, norm_g[layer, 0], cm[0], cm[1], w_in[:, cd_kv:], mc)
            dq = head_prep(p, 3 * c_w, d_vw, rope=rope, scale=scale * LOG2E, seq=seq)
            dk = head_prep(p, cd_kv, d_vw, rope=rope, seq=seq)
            k_all = jnp.concatenate([pc[:, :d_vw].reshape(batch, ctx_len, d_vw), dk.reshape(batch, seq, d_vw)], axis=1)
            v_all = jnp.concatenate([pc[:, d_vw:].reshape(batch, ctx_len, d_vw),
                                     p[:, cd_kv + d_vw:].reshape(batch, seq, d_vw)], axis=1)
            y_c = short_conv(p, sconv_w[i], seq)
            o_d = flash_diff(dq, k_all, v_all, (diff_lq1[i], diff_lk1[i], diff_lq2[i], diff_lk2[i]),
                             diff_subln_g[i], lam_init, batch, d_heads)
            x = out_proj_residual(x, y_c, o_d, cd_w_out[i].astype(BF16), lat[2], seq)
        ffn_w = (ffn_w_gate[layer].astype(BF16), ffn_w_up[layer].astype(BF16), ffn_conv_w[layer],
                 ffn_conv_b[layer], ffn_w_down[layer].astype(BF16))
        x = conv_glu_residual(x, norm_g[layer, 1], lat[3], lat[4], lat[5], *ffn_w, seq, seq,
                              final_gain=final_g if last else None)
        if not last:
            h_ctx = conv_glu_residual(h_ctx, norm_g[layer, 1], cm[3], cm[4], cm[5], *ffn_w, ctx_len, mc)
    return x.reshape(batch, seq, d)
```
